```python
import jax, jax.numpy as jnp
from jax import lax
import numpy as np

D_MODEL = 1024
BATCH = 8
SEQ = 8192
DEPTH = 2

HEAD_DIM = 64
POOL_WIDTH = D_MODEL // 4
POOL_WINDOWS = (2, 4, 8, 16)
N_POOL_GROUPS = len(POOL_WINDOWS)
POOL_GROUP_DIM = POOL_WIDTH // N_POOL_GROUPS
CONV_WIDTH = (3 * D_MODEL) // 8
CONV_KERNEL = 31
SGU_WIDTH = D_MODEL - POOL_WIDTH - CONV_WIDTH
SGU_HEADS = SGU_WIDTH // HEAD_DIM
CHUNK = 128
D_MIX = POOL_WIDTH + CONV_WIDTH + SGU_WIDTH
D_IN = POOL_WIDTH + 2 * CONV_WIDTH + 2 * SGU_WIDTH
D_FF = ((8 * D_MODEL // 3 + 127) // 128) * 128
N_EXPERTS = 8
TOP_K = 2
D_FF_EXPERT = 7 * D_MODEL // 2
N_MOD = 6
EPS = 1e-6

kernel_name = "hybrid_pool_conv_sgu_moe_encoder"


def rms_norm(x, g):
    xf = x.astype(jnp.float32)
    y = xf * lax.rsqrt(jnp.mean(xf * xf, axis=-1, keepdims=True) + EPS)
    return (y * g.astype(jnp.float32)).astype(x.dtype)


def layer_norm(x, g, b):
    xf = x.astype(jnp.float32)
    mu = jnp.mean(xf, axis=-1, keepdims=True)
    var = jnp.mean(jnp.square(xf - mu), axis=-1, keepdims=True)
    y = (xf - mu) * lax.rsqrt(var + EPS) * g.astype(jnp.float32) + b.astype(jnp.float32)
    return y.astype(x.dtype)


def multiscale_pool(a, pool_w, pool_scale):
    S = a.shape[1]
    af = a.astype(jnp.float32)
    cs = jnp.pad(jnp.cumsum(af, axis=1), ((0, 0), (1, 0), (0, 0)))
    t = jnp.arange(S)
    outs = []
    for g, w in enumerate(POOL_WINDOWS):
        half = w // 2
        hi = jnp.clip(t + half, 0, S)
        lo = jnp.clip(t - half, 0, S)
        sl = slice(g * POOL_GROUP_DIM, (g + 1) * POOL_GROUP_DIM)
        cs_g = cs[:, :, sl]
        win_sum = jnp.take(cs_g, hi, axis=1) - jnp.take(cs_g, lo, axis=1)
        count = (hi - lo).astype(jnp.float32)[None, :, None]
        diff = (win_sum / count - af[:, :, sl]).astype(a.dtype)
        outs.append(jnp.einsum('bsc,cd->bsd', diff, pool_w[g]))
    return jnp.concatenate(outs, axis=-1) * pool_scale


def conformer_conv(p, conv_w, conv_b, ln_g, ln_b):
    val, gate = jnp.split(p, 2, axis=-1)
    glu = val * jax.nn.sigmoid(gate)
    y = lax.conv_general_dilated(
        glu, conv_w[:, None, :], window_strides=(1,),
        padding=[(CONV_KERNEL // 2, CONV_KERNEL // 2)],
        dimension_numbers=('NWC', 'WIO', 'NWC'),
        feature_group_count=CONV_WIDTH) + conv_b
    return jax.nn.silu(layer_norm(y, ln_g, ln_b))


def spatial_gating(p, ln_g, ln_b, sgu_w, sgu_b):
    B, S, _ = p.shape
    u, v = jnp.split(p, 2, axis=-1)
    v = layer_norm(v, ln_g, ln_b).reshape(B, S // CHUNK, CHUNK, SGU_HEADS, HEAD_DIM)
    mixed = jnp.einsum('hpq,bnqhd->bnphd', sgu_w, v) + sgu_b.T[:, :, None]
    return u * mixed.reshape(B, S, SGU_WIDTH)


def swiglu(h, w_gate, w_up, w_down):
    return (jax.nn.silu(h @ w_gate) * (h @ w_up)) @ w_down


def moe_swiglu(h, router_w, router_b, w_gate, w_up, w_down):
    logits = jnp.einsum('bsd,de->bse', h.astype(jnp.float32), router_w.astype(jnp.float32)) \
        + router_b.astype(jnp.float32)
    top_vals, top_idx = lax.top_k(logits, TOP_K)
    top_w = jax.nn.softmax(top_vals, axis=-1)
    combine = jnp.sum(jax.nn.one_hot(top_idx, N_EXPERTS, dtype=jnp.float32) * top_w[..., None], axis=-2)
    combine = combine.astype(h.dtype)
    y = jnp.zeros_like(h)
    for e in range(N_EXPERTS):
        y = y + combine[..., e:e + 1] * swiglu(h, w_gate[e], w_up[e], w_down[e])
    return y


def setup_inputs(seed: int = 0) -> dict:
    key = jax.random.key(seed)
    ks = jax.random.split(key, 32)
    n_dense = (DEPTH + 1) // 2
    n_moe = DEPTH // 2

    def nrm(k, shape, fan_in):
        return jax.random.normal(k, shape, jnp.float32) * (fan_in ** -0.5)

    def near_one(k, shape):
        return 1.0 + 0.1 * jax.random.normal(k, shape, jnp.float32)

    def small(k, shape):
        return 0.02 * jax.random.normal(k, shape, jnp.float32)

    return {
        "x": jax.random.normal(ks[0], (BATCH, SEQ, D_MODEL), jnp.float32),
        "c": jax.random.normal(ks[1], (BATCH, D_MODEL), jnp.float32),
        "ada_w": nrm(ks[2], (DEPTH, D_MODEL, N_MOD * D_MODEL), D_MODEL),
        "ada_b": small(ks[3], (DEPTH, N_MOD * D_MODEL)),
        "mix_norm_g": near_one(ks[4], (DEPTH, D_MODEL)),
        "ffn_norm_g": near_one(ks[5], (DEPTH, D_MODEL)),
        "w_in": nrm(ks[6], (DEPTH, D_MODEL, D_IN), D_MODEL),
        "pool_w": nrm(ks[7], (DEPTH, N_POOL_GROUPS, POOL_GROUP_DIM, POOL_GROUP_DIM), POOL_GROUP_DIM),
        "pool_scale": near_one(ks[8], (DEPTH, POOL_WIDTH)),
        "conv_w": nrm(ks[9], (DEPTH, CONV_KERNEL, CONV_WIDTH), CONV_KERNEL),
        "conv_b": small(ks[10], (DEPTH, CONV_WIDTH)),
        "conv_ln_g": near_one(ks[11], (DEPTH, CONV_WIDTH)),
        "conv_ln_b": small(ks[12], (DEPTH, CONV_WIDTH)),
        "sgu_ln_g": near_one(ks[13], (DEPTH, SGU_WIDTH)),
        "sgu_ln_b": small(ks[14], (DEPTH, SGU_WIDTH)),
        "sgu_w": nrm(ks[15], (DEPTH, SGU_HEADS, CHUNK, CHUNK), CHUNK),
        "sgu_b": near_one(ks[16], (DEPTH, SGU_HEADS, CHUNK)),
        "w_out": nrm(ks[17], (DEPTH, D_MIX, D_MODEL), D_MIX),
        "ffn_w_gate": nrm(ks[18], (n_dense, D_MODEL, D_FF), D_MODEL),
        "ffn_w_up": nrm(ks[19], (n_dense, D_MODEL, D_FF), D_MODEL),
        "ffn_w_down": nrm(ks[20], (n_dense, D_FF, D_MODEL), D_FF),
        "router_w": nrm(ks[21], (n_moe, D_MODEL, N_EXPERTS), D_MODEL),
        "router_b": 0.01 * jax.random.normal(ks[22], (n_moe, N_EXPERTS), jnp.float32),
        "moe_w_gate": nrm(ks[23], (n_moe, N_EXPERTS, D_MODEL, D_FF_EXPERT), D_MODEL),
        "moe_w_up": nrm(ks[24], (n_moe, N_EXPERTS, D_MODEL, D_FF_EXPERT), D_MODEL),
        "moe_w_down": nrm(ks[25], (n_moe, N_EXPERTS, D_FF_EXPERT, D_MODEL), D_FF_EXPERT),
        "final_norm_g": near_one(ks[26], (D_MODEL,)),
    }


def reference(x, c, ada_w, ada_b, mix_norm_g, ffn_norm_g, w_in, pool_w, pool_scale,
              conv_w, conv_b, conv_ln_g, conv_ln_b, sgu_ln_g, sgu_ln_b, sgu_w, sgu_b, w_out,
              ffn_w_gate, ffn_w_up, ffn_w_down, router_w, router_b,
              moe_w_gate, moe_w_up, moe_w_down, final_norm_g):
    cond = jax.nn.silu(c)
    for l in range(DEPTH):
        mod = cond @ ada_w[l] + ada_b[l]
        mods = jnp.split(mod[:, None, :], N_MOD, axis=-1)
        shift_m, scale_m, gate_m, shift_f, scale_f, gate_f = mods

        h = rms_norm(x, mix_norm_g[l]) * (1.0 + scale_m) + shift_m
        proj = h @ w_in[l]
        p_pool, p_conv, p_sgu = jnp.split(proj, [POOL_WIDTH, POOL_WIDTH + 2 * CONV_WIDTH], axis=-1)
        out_a = multiscale_pool(p_pool, pool_w[l], pool_scale[l])
        out_b = conformer_conv(p_conv, conv_w[l], conv_b[l], conv_ln_g[l], conv_ln_b[l])
        out_c = spatial_gating(p_sgu, sgu_ln_g[l], sgu_ln_b[l], sgu_w[l], sgu_b[l])
        mixed = jnp.concatenate([out_a, out_b, out_c], axis=-1)
        x = x + gate_m * (mixed @ w_out[l])

        h = rms_norm(x, ffn_norm_g[l]) * (1.0 + scale_f) + shift_f
        i = l // 2
        if l % 2 == 0:
            y = swiglu(h, ffn_w_gate[i], ffn_w_up[i], ffn_w_down[i])
        else:
            y = moe_swiglu(h, router_w[i], router_b[i], moe_w_gate[i], moe_w_up[i], moe_w_down[i])
        x = x + gate_f * y
    return rms_norm(x, final_norm_g)
```

```python
import functools

import jax
import jax.numpy as jnp
from jax import lax
from jax.experimental import pallas as pl
from jax.experimental.pallas import tpu as pltpu

EPS = 1e-6
F32 = jnp.float32
BF16 = jnp.bfloat16

SUBLANES = 8
LANES = 128

POOL_WINDOWS = (2, 4, 8, 16)
POOL_GROUP = 64
CONV_KERNEL = 31
HEAD_DIM = 64
CHUNK = 128
N_EXPERTS = 8
HALO = 16

MIX_TM = 512
FFN_TM = 512
FFN_FC = 768
ROUTE_T = 512
MOE_TM = 1024
MOE_FC = 512
CONV_RC = 64
VMEM_LIMIT = 48 * 1024 * 1024


def _dot(a, b):
    return jnp.dot(a, b, preferred_element_type=F32)


def _dot_tn(a, b):
    return lax.dot_general(a, b, (((0,), (0,)), ((), ())), preferred_element_type=F32)


def _sigmoid(x):
    return 1.0 / (1.0 + jnp.exp(-x))


def _rms_mod(x, g_row, scale_row, shift_row):
    ms = jnp.mean(x * x, axis=-1, keepdims=True)
    return x * lax.rsqrt(ms + EPS) * (g_row * (1.0 + scale_row)) + shift_row


def _layer_norm(x, g_row, b_row):
    mu = jnp.mean(x, axis=-1, keepdims=True)
    xc = x - mu
    var = jnp.mean(xc * xc, axis=-1, keepdims=True)
    return xc * lax.rsqrt(var + EPS) * g_row + b_row


def _mod_kernel(c_ref, w_ref, b_ref, o_ref):
    c = c_ref[...]
    cond = (c * _sigmoid(c)).astype(BF16)
    o_ref[0] = _dot(cond, w_ref[0].astype(BF16)) + b_ref[0]


def _modulation(c, ada_w, ada_b):
    depth, d, nd = ada_w.shape
    b = c.shape[0]
    nblk = nd // d
    return pl.pallas_call(
        _mod_kernel,
        grid=(depth, nblk),
        in_specs=[
            pl.BlockSpec((b, d), lambda l, n: (0, 0)),
            pl.BlockSpec((1, d, d), lambda l, n: (l, 0, n)),
            pl.BlockSpec((1, 1, d), lambda l, n: (l, 0, n)),
        ],
        out_specs=pl.BlockSpec((1, b, d), lambda l, n: (l, 0, n)),
        out_shape=jax.ShapeDtypeStruct((depth, b, nd), F32),
        name="mod",
    )(c, ada_w, ada_b.reshape(depth, 1, nd))


def _mix_kernel(xm_ref, xp_ref, xn_ref, mod_ref, g1_ref, g2_ref, win_ref, pbd_ref, pscale_ref,
                cw_ref, cb_ref, clg_ref, clb_ref, slg_ref, slb_ref, swcat_ref, sbias_ref, wout_ref,
                xo_ref, ho_ref, glu_ref, mixed_ref, *, tm, seq):
    j = pl.program_id(1)
    nj = pl.num_programs(1)
    ext = tm + 2 * HALO
    pool_w = pbd_ref.shape[0]
    conv_w = cw_ref.shape[1]
    sgu_w = slg_ref.shape[1]
    pc_w = pool_w + 2 * conv_w

    mod = mod_ref[0]
    shift_m, scale_m, gate_m = mod[0:1], mod[1:2], mod[2:3]
    shift_f, scale_f = mod[3:4], mod[4:5]

    xm = xm_ref[0]
    xe = jnp.concatenate([xp_ref[0], xm, xn_ref[0]], axis=0)
    hn = _rms_mod(xe, g1_ref[...], scale_m, shift_m)
    r = lax.broadcasted_iota(jnp.int32, (ext, 1), 0)
    ok = ((r >= HALO) | (j > 0)) & ((r < HALO + tm) | (j < nj - 1))
    hb = jnp.where(ok, hn, 0.0).astype(BF16)

    p1 = _dot(hb, win_ref[:, 0:pc_w])
    p2 = _dot(hb[HALO:HALO + tm], win_ref[:, pc_w:pc_w + 2 * sgu_w])

    a = p1[:, 0:pool_w]
    s2 = a[0:ext - 1] + a[1:ext]
    s4 = s2[0:ext - 3] + s2[2:ext - 1]
    a_hi = s4[:, LANES:pool_w]
    s8 = a_hi[0:ext - 7] + a_hi[4:ext - 3]
    s16 = s8[0:ext - 15] + s8[8:ext - 7]
    lane = lax.broadcasted_iota(jnp.int32, (tm, LANES), 1)
    first = lane < POOL_GROUP
    win_lo = jnp.where(first, s2[HALO - 1:HALO - 1 + tm, 0:LANES], s4[HALO - 2:HALO - 2 + tm, 0:LANES])
    win_hi = jnp.where(first, s8[HALO - 4:HALO - 4 + tm], s16[HALO - 8:HALO - 8 + tm])
    win = jnp.concatenate([win_lo, win_hi], axis=1)
    t = lax.broadcasted_iota(jnp.int32, (tm, pool_w), 0) + j * tm
    lane2 = lax.broadcasted_iota(jnp.int32, (tm, pool_w), 1)
    half = jnp.where(lane2 < POOL_GROUP, 1,
                     jnp.where(lane2 < 2 * POOL_GROUP, 2, jnp.where(lane2 < 3 * POOL_GROUP, 4, 8)))
    count = (jnp.minimum(t + half, seq) - jnp.maximum(t - half, 0)).astype(F32)
    diff = win / count - a[HALO:HALO + tm]
    out_a = _dot(diff.astype(BF16), pbd_ref[...]) * pscale_ref[...]
    mixed_ref[:, 0:pool_w] = out_a.astype(BF16)

    val = p1[:, pool_w:pool_w + conv_w]
    gate = p1[:, pool_w + conv_w:pc_w]
    glu_ref[...] = val * _sigmoid(gate)
    cb = cb_ref[...]
    clg = clg_ref[...]
    clb = clb_ref[...]
    for c in range(tm // CONV_RC):
        base = c * CONV_RC + HALO - CONV_KERNEL // 2
        acc = glu_ref[base:base + CONV_RC, :] * cw_ref[0:1, :]
        for k in range(1, CONV_KERNEL):
            acc = acc + glu_ref[base + k:base + k + CONV_RC, :] * cw_ref[k:k + 1, :]
        yn = _layer_norm(acc + cb, clg, clb)
        mixed_ref[c * CONV_RC:(c + 1) * CONV_RC, pool_w:pool_w + conv_w] = (yn * _sigmoid(yn)).astype(BF16)

    u = p2[:, 0:sgu_w]
    v = _layer_norm(p2[:, sgu_w:2 * sgu_w], slg_ref[...], slb_ref[...])
    lane3 = lax.broadcasted_iota(jnp.int32, (tm, sgu_w), 1)
    is_lo = (lane3 % LANES) < HEAD_DIM
    v_lo = jnp.where(is_lo, v, 0.0).astype(BF16)
    v_hi = jnp.where(is_lo, 0.0, v).astype(BF16)
    sbias = sbias_ref[...]
    col0 = pool_w + conv_w
    for cp in range(tm // (2 * CHUNK)):
        ra = 2 * cp * CHUNK
        rb = ra + CHUNK
        for m in range(sgu_w // LANES):
            ls = slice(m * LANES, (m + 1) * LANES)
            rhs = jnp.concatenate([
                jnp.concatenate([v_lo[ra:ra + CHUNK, ls], v_lo[rb:rb + CHUNK, ls]], axis=1),
                jnp.concatenate([v_hi[ra:ra + CHUNK, ls], v_hi[rb:rb + CHUNK, ls]], axis=1)], axis=0)
            mixed = _dot(swcat_ref[m], rhs)
            bias = sbias[:, ls]
            oa = u[ra:ra + CHUNK, ls] * (mixed[:, 0:LANES] + bias)
            ob = u[rb:rb + CHUNK, ls] * (mixed[:, LANES:2 * LANES] + bias)
            mixed_ref[ra:ra + CHUNK, col0 + m * LANES:col0 + (m + 1) * LANES] = oa.astype(BF16)
            mixed_ref[rb:rb + CHUNK, col0 + m * LANES:col0 + (m + 1) * LANES] = ob.astype(BF16)

    xnew = xm + gate_m * _dot(mixed_ref[...], wout_ref[...])
    xo_ref[0] = xnew
    ho_ref[0] = _rms_mod(xnew, g2_ref[...], scale_f, shift_f).astype(BF16)


def _mix_layer(x, mod, g1, g2, w_in, pool_bd, pool_scale, conv_w, conv_b, conv_ln_g, conv_ln_b,
               sgu_ln_g, sgu_ln_b, sgu_wcat, sgu_bias, w_out):
    b, s, d = x.shape
    tm = MIX_TM
    assert s % tm == 0 and tm % (2 * CHUNK) == 0 and tm % HALO == 0
    hb = tm // HALO
    last_halo = s // HALO - 1
    full = lambda arr: pl.BlockSpec(arr.shape, lambda bi, j: (0,) * arr.ndim)
    params = [g1, g2, w_in, pool_bd, pool_scale, conv_w, conv_b, conv_ln_g, conv_ln_b,
              sgu_ln_g, sgu_ln_b, sgu_wcat, sgu_bias, w_out]
    return pl.pallas_call(
        functools.partial(_mix_kernel, tm=tm, seq=s),
        grid=(b, s // tm),
        in_specs=[
            pl.BlockSpec((1, tm, d), lambda bi, j: (bi, j, 0)),
            pl.BlockSpec((1, HALO, d), lambda bi, j: (bi, jnp.maximum(j * hb - 1, 0), 0)),
            pl.BlockSpec((1, HALO, d), lambda bi, j: (bi, jnp.minimum((j + 1) * hb, last_halo), 0)),
            pl.BlockSpec((1,) + mod.shape[1:], lambda bi, j: (bi, 0, 0)),
        ] + [full(p) for p in params],
        out_specs=[
            pl.BlockSpec((1, tm, d), lambda bi, j: (bi, j, 0)),
            pl.BlockSpec((1, tm, d), lambda bi, j: (bi, j, 0)),
        ],
        out_shape=[jax.ShapeDtypeStruct((b, s, d), F32), jax.ShapeDtypeStruct((b, s, d), BF16)],
        scratch_shapes=[
            pltpu.VMEM((tm + 2 * HALO, conv_w.shape[1]), F32),
            pltpu.VMEM((tm, d), BF16),
        ],
        compiler_params=pltpu.CompilerParams(
            dimension_semantics=("arbitrary", "arbitrary"), vmem_limit_bytes=VMEM_LIMIT),
        name="mix",
    )(x, x, x, mod, *params)


def _ffn_kernel(x_ref, h_ref, mod_ref, wg_ref, wu_ref, wd_ref, o_ref, *, fc):
    h = h_ref[0]
    gate_f = mod_ref[0][5:6]
    width = wg_ref.shape[1]
    acc = None
    for lo in range(0, width, fc):
        fs = slice(lo, min(lo + fc, width))
        g = _dot(h, wg_ref[:, fs])
        u = _dot(h, wu_ref[:, fs])
        act = (g * _sigmoid(g) * u).astype(BF16)
        part = _dot(act, wd_ref[fs, :])
        acc = part if acc is None else acc + part
    o_ref[0] = x_ref[0] + gate_f * acc


def _dense_ffn(x, h, mod, wg, wu, wd):
    b, s, d = x.shape
    tm = FFN_TM
    assert s % tm == 0
    full = lambda arr: pl.BlockSpec(arr.shape, lambda bi, j: (0,) * arr.ndim)
    return pl.pallas_call(
        functools.partial(_ffn_kernel, fc=FFN_FC),
        grid=(b, s // tm),
        in_specs=[
            pl.BlockSpec((1, tm, d), lambda bi, j: (bi, j, 0)),
            pl.BlockSpec((1, tm, d), lambda bi, j: (bi, j, 0)),
            pl.BlockSpec((1,) + mod.shape[1:], lambda bi, j: (bi, 0, 0)),
            full(wg), full(wu), full(wd),
        ],
        out_specs=pl.BlockSpec((1, tm, d), lambda bi, j: (bi, j, 0)),
        out_shape=jax.ShapeDtypeStruct((b, s, d), F32),
        compiler_params=pltpu.CompilerParams(
            dimension_semantics=("arbitrary", "arbitrary"), vmem_limit_bytes=VMEM_LIMIT),
        name="ffn",
    )(x, h, mod, wg, wu, wd)


def _route_kernel(h_ref, rw_ref, rb_ref, tri_ref, route_ref, cnt_ref):
    t = h_ref.shape[0]
    logits = _dot(h_ref[...], rw_ref[...])
    lt = logits.T[0:N_EXPERTS] + rb_ref[...]
    eid = lax.broadcasted_iota(jnp.int32, (N_EXPERTS, t), 0).astype(F32)
    m1 = jnp.max(lt, axis=0, keepdims=True)
    i1 = jnp.min(jnp.where(lt == m1, eid, float(N_EXPERTS)), axis=0, keepdims=True)
    sel1 = eid == i1
    lt2 = jnp.where(sel1, -jnp.inf, lt)
    m2 = jnp.max(lt2, axis=0, keepdims=True)
    i2 = jnp.min(jnp.where(lt2 == m2, eid, float(N_EXPERTS)), axis=0, keepdims=True)
    sel2 = eid == i2
    e21 = jnp.exp(m2 - m1)
    w1 = 1.0 / (1.0 + e21)
    w2 = e21 / (1.0 + e21)
    count = jnp.where(sel1 | sel2, 1.0, 0.0)
    rank = _dot(count.astype(BF16), tri_ref[...])
    r1 = jnp.sum(jnp.where(sel1, rank, 0.0), axis=0, keepdims=True)
    r2 = jnp.sum(jnp.where(sel2, rank, 0.0), axis=0, keepdims=True)
    out = jnp.zeros((N_EXPERTS, t), F32)
    for k, rowv in enumerate((i1, i2, w1, w2, r1, r2)):
        out = jnp.where(eid == float(k), rowv, out)
    route_ref[0] = out
    cnt_ref[0] = jnp.broadcast_to(jnp.sum(count, axis=1, keepdims=True), (N_EXPERTS, LANES))


def _route(h2d, rw_pad, rb_col, tri):
    n, d = h2d.shape
    t = ROUTE_T
    nt = n // t
    return pl.pallas_call(
        _route_kernel,
        grid=(nt,),
        in_specs=[
            pl.BlockSpec((t, d), lambda i: (i, 0)),
            pl.BlockSpec(rw_pad.shape, lambda i: (0, 0)),
            pl.BlockSpec(rb_col.shape, lambda i: (0, 0)),
            pl.BlockSpec(tri.shape, lambda i: (0, 0)),
        ],
        out_specs=[
            pl.BlockSpec((1, N_EXPERTS, t), lambda i: (i, 0, 0)),
            pl.BlockSpec((1, N_EXPERTS, LANES), lambda i: (i, 0, 0)),
        ],
        out_shape=[jax.ShapeDtypeStruct((nt, N_EXPERTS, t), F32),
                   jax.ShapeDtypeStruct((nt, N_EXPERTS, LANES), F32)],
        compiler_params=pltpu.CompilerParams(dimension_semantics=("arbitrary",)),
        name="route",
    )(h2d, rw_pad, rb_col, tri)


def _dest_rows(route, seg_ref, i):
    i1, i2, r1, r2 = route[0:1], route[1:2], route[4:5], route[5:6]
    s1 = jnp.zeros_like(i1)
    s2 = jnp.zeros_like(i2)
    for e in range(N_EXPERTS):
        start = seg_ref[i * N_EXPERTS + e].astype(F32)
        s1 = jnp.where(i1 == e, start, s1)
        s2 = jnp.where(i2 == e, start, s2)
    return (s1 + r1).astype(jnp.int32), (s2 + r2).astype(jnp.int32)


def _segment_copies(i, seg_ref, cnt_ref, off_ref, vmem_ref, hbm_ref, sem, t, to_hbm, wait):
    nbits = (t // SUBLANES).bit_length()
    for e in range(N_EXPERTS):
        seg = seg_ref[i * N_EXPERTS + e]
        cnt = cnt_ref[i * N_EXPERTS + e]
        off = off_ref[i * N_EXPERTS + e]
        for k in range(nbits):
            size = SUBLANES << k
            done = (cnt >> (k + 4)) << (k + 4)

            @pl.when(((cnt >> (k + 3)) & 1) == 1)
            def _():
                v = vmem_ref.at[pl.ds(pl.multiple_of(seg + done, SUBLANES), size)]
                g = hbm_ref.at[pl.ds(pl.multiple_of(off + done, SUBLANES), size)]
                cp = pltpu.make_async_copy(v, g, sem) if to_hbm else pltpu.make_async_copy(g, v, sem)
                if wait:
                    cp.wait()
                else:
                    cp.start()


def _dispatch_kernel(seg_ref, cnt_ref, off_ref, h_ref, route_ref, hs_ref, comp_ref, sem):
    i = pl.program_id(0)
    t = h_ref.shape[0]
    rows = comp_ref.shape[0]
    d1, d2 = _dest_rows(route_ref[0], seg_ref, i)
    rid = lax.broadcasted_iota(jnp.int32, (rows, t), 0)
    onehot = jnp.where((rid == d1) | (rid == d2), 1.0, 0.0).astype(BF16)
    comp_ref[...] = _dot(onehot, h_ref[...])
    _segment_copies(i, seg_ref, cnt_ref, off_ref, comp_ref, hs_ref, sem, t, True, False)
    _segment_copies(i, seg_ref, cnt_ref, off_ref, comp_ref, hs_ref, sem, t, True, True)


def _dispatch(h2d, route, seg, cnt8, off, total_rows):
    n, d = h2d.shape
    t = ROUTE_T
    nt = n // t
    rows = 2 * t + N_EXPERTS * SUBLANES
    return pl.pallas_call(
        _dispatch_kernel,
        grid_spec=pltpu.PrefetchScalarGridSpec(
            num_scalar_prefetch=3,
            grid=(nt,),
            in_specs=[
                pl.BlockSpec((t, d), lambda i, *_: (i, 0)),
                pl.BlockSpec((1, N_EXPERTS, t), lambda i, *_: (i, 0, 0)),
            ],
            out_specs=pl.BlockSpec(memory_space=pl.ANY),
            scratch_shapes=[pltpu.VMEM((rows, d), F32), pltpu.SemaphoreType.DMA(())],
        ),
        out_shape=jax.ShapeDtypeStruct((total_rows, d), F32),
        compiler_params=pltpu.CompilerParams(
            dimension_semantics=("arbitrary",), vmem_limit_bytes=VMEM_LIMIT),
        name="dispatch",
    )(seg, cnt8, off, h2d, route)


def _moe_kernel(be_ref, nv_ref, hs_ref, wg_ref, wu_ref, wd_ref, o_ref, hb_ref, acc_ref):
    g = pl.program_id(0)
    f = pl.program_id(1)
    nvalid = nv_ref[g]

    @pl.when(nvalid > 0)
    def _():
        @pl.when(f == 0)
        def _():
            rid = lax.broadcasted_iota(jnp.int32, (hs_ref.shape[0], 1), 0)
            hb_ref[...] = jnp.where(rid < nvalid, hs_ref[...], 0.0).astype(BF16)

        h = hb_ref[...]
        gt = _dot(h, wg_ref[0])
        up = _dot(h, wu_ref[0])
        act = (gt * _sigmoid(gt) * up).astype(BF16)
        part = _dot(act, wd_ref[0])

        @pl.when(f == 0)
        def _():
            acc_ref[...] = part

        @pl.when(f > 0)
        def _():
            acc_ref[...] += part

        @pl.when(f == pl.num_programs(1) - 1)
        def _():
            o_ref[...] = acc_ref[...]


def _moe(hs, blk_e, blk_nv, wg, wu, wd):
    rows, d = hs.shape
    tm = MOE_TM
    fc = MOE_FC
    nf = wg.shape[2] // fc
    nblk = rows // tm

    def fidx(g, f, nv):
        return jnp.where(nv[g] > 0, f, nf - 1)

    return pl.pallas_call(
        _moe_kernel,
        grid_spec=pltpu.PrefetchScalarGridSpec(
            num_scalar_prefetch=2,
            grid=(nblk, nf),
            in_specs=[
                pl.BlockSpec((tm, d), lambda g, f, be, nv: (g, 0)),
                pl.BlockSpec((1, d, fc), lambda g, f, be, nv: (be[g], 0, fidx(g, f, nv))),
                pl.BlockSpec((1, d, fc), lambda g, f, be, nv: (be[g], 0, fidx(g, f, nv))),
                pl.BlockSpec((1, fc, d), lambda g, f, be, nv: (be[g], fidx(g, f, nv), 0)),
            ],
            out_specs=pl.BlockSpec((tm, d), lambda g, f, be, nv: (g, 0)),
            scratch_shapes=[pltpu.VMEM((tm, d), BF16), pltpu.VMEM((tm, d), F32)],
        ),
        out_shape=jax.ShapeDtypeStruct((rows, d), F32),
        compiler_params=pltpu.CompilerParams(
            dimension_semantics=("arbitrary", "arbitrary"), vmem_limit_bytes=VMEM_LIMIT),
        name="moe",
    )(blk_e, blk_nv, hs, wg, wu, wd)


def _combine_kernel(seg_ref, cnt_ref, off_ref, x_ref, route_ref, mod_ref, gfin_ref, ys_ref, o_ref,
                    comp_ref, sem):
    i = pl.program_id(0)
    t = x_ref.shape[0]
    rows = comp_ref.shape[0]

    @pl.when(i == 0)
    def _():
        comp_ref[...] = jnp.zeros_like(comp_ref)

    _segment_copies(i, seg_ref, cnt_ref, off_ref, comp_ref, ys_ref, sem, t, False, False)
    route = route_ref[0]
    d1, d2 = _dest_rows(route, seg_ref, i)
    w1, w2 = route[2:3], route[3:4]
    rid = lax.broadcasted_iota(jnp.int32, (rows, t), 0)
    wmat = (jnp.where(rid == d1, w1, 0.0) + jnp.where(rid == d2, w2, 0.0)).astype(BF16)
    _segment_copies(i, seg_ref, cnt_ref, off_ref, comp_ref, ys_ref, sem, t, False, True)
    y = _dot_tn(wmat, comp_ref[...].astype(BF16))
    gate_f = mod_ref[0][5:6]
    x2 = x_ref[...] + gate_f * y
    ms = jnp.mean(x2 * x2, axis=-1, keepdims=True)
    o_ref[...] = x2 * lax.rsqrt(ms + EPS) * gfin_ref[...]


def _combine(x2d, route, mod, gfin, ys, seg, cnt8, off, seq):
    n, d = x2d.shape
    t = ROUTE_T
    nt = n // t
    tiles_per_seq = seq // t
    rows = 2 * t + N_EXPERTS * SUBLANES
    return pl.pallas_call(
        _combine_kernel,
        grid_spec=pltpu.PrefetchScalarGridSpec(
            num_scalar_prefetch=3,
            grid=(nt,),
            in_specs=[
                pl.BlockSpec((t, d), lambda i, *_: (i, 0)),
                pl.BlockSpec((1, N_EXPERTS, t), lambda i, *_: (i, 0, 0)),
                pl.BlockSpec((1,) + mod.shape[1:], lambda i, *_: (i // tiles_per_seq, 0, 0)),
                pl.BlockSpec(gfin.shape, lambda i, *_: (0, 0)),
                pl.BlockSpec(memory_space=pl.ANY),
            ],
            out_specs=pl.BlockSpec((t, d), lambda i, *_: (i, 0)),
            scratch_shapes=[pltpu.VMEM((rows, d), F32), pltpu.SemaphoreType.DMA(())],
        ),
        out_shape=jax.ShapeDtypeStruct((n, d), F32),
        compiler_params=pltpu.CompilerParams(
            dimension_semantics=("arbitrary",), vmem_limit_bytes=VMEM_LIMIT),
        name="combine",
    )(seg, cnt8, off, x2d, route, mod, gfin, ys)


def _moe_plan(cnt, tm):
    nt = cnt.shape[0]
    cnt8 = (cnt + SUBLANES - 1) // SUBLANES * SUBLANES
    seg = jnp.cumsum(cnt8, axis=1) - cnt8
    tot = jnp.sum(cnt8, axis=0)
    nblk = (tot + tm - 1) // tm
    blk_end = jnp.cumsum(nblk)
    blk_start = blk_end - nblk
    off = (blk_start * tm)[None, :] + jnp.cumsum(cnt8, axis=0) - cnt8
    max_rows = 2 * nt * ROUTE_T + nt * N_EXPERTS * (SUBLANES - 1)
    n_blocks = (max_rows + tm - 1) // tm + N_EXPERTS
    g = jnp.arange(n_blocks, dtype=jnp.int32)
    blk_e = jnp.minimum(jnp.searchsorted(blk_end, g, side="right"), N_EXPERTS - 1).astype(jnp.int32)
    blk_nv = jnp.clip(tot[blk_e] - (g - blk_start[blk_e]) * tm, 0, tm).astype(jnp.int32)
    flat = lambda a: a.reshape(-1).astype(jnp.int32)
    return flat(seg), flat(cnt8), flat(off), blk_e, blk_nv, n_blocks * tm


def kernel(x, c, ada_w, ada_b, mix_norm_g, ffn_norm_g, w_in, pool_w, pool_scale, conv_w, conv_b,
           conv_ln_g, conv_ln_b, sgu_ln_g, sgu_ln_b, sgu_w, sgu_b, w_out, ffn_w_gate, ffn_w_up,
           ffn_w_down, router_w, router_b, moe_w_gate, moe_w_up, moe_w_down, final_norm_g):
    b, s, d = x.shape
    depth = ada_w.shape[0]
    n_mod = ada_w.shape[2] // d
    mod_all = _modulation(c, ada_w, ada_b).reshape(depth, b, n_mod, d)
    row = lambda v: v.reshape(1, -1)

    for l in range(depth):
        mod = mod_all[l]
        pool_bd = jax.scipy.linalg.block_diag(*[pool_w[l, g] for g in range(pool_w.shape[1])]).astype(BF16)
        sw = sgu_w[l]
        sgu_wcat = jnp.concatenate([sw[0::2], sw[1::2]], axis=2).astype(BF16)
        sgu_bias = jnp.repeat(sgu_b[l].T, HEAD_DIM, axis=1)
        x, h = _mix_layer(x, mod, row(mix_norm_g[l]), row(ffn_norm_g[l]), w_in[l].astype(BF16), pool_bd,
                          row(pool_scale[l]), conv_w[l], row(conv_b[l]), row(conv_ln_g[l]),
                          row(conv_ln_b[l]), row(sgu_ln_g[l]), row(sgu_ln_b[l]), sgu_wcat, sgu_bias,
                          w_out[l].astype(BF16))
        i = l // 2
        if l % 2 == 0:
            x = _dense_ffn(x, h, mod, ffn_w_gate[i].astype(BF16), ffn_w_up[i].astype(BF16),
                           ffn_w_down[i].astype(BF16))
        else:
            assert l == depth - 1, "the combine kernel applies the final norm"
            h2d = h.reshape(b * s, d)
            rw_pad = jnp.pad(router_w[i], ((0, 0), (0, LANES - N_EXPERTS))).astype(BF16)
            tri = jnp.triu(jnp.ones((ROUTE_T, ROUTE_T), BF16), k=1)
            route, cnt = _route(h2d, rw_pad, router_b[i].reshape(N_EXPERTS, 1), tri)
            seg, cnt8, off, blk_e, blk_nv, total_rows = _moe_plan(cnt[:, :, 0].astype(jnp.int32), MOE_TM)
            hs = _dispatch(h2d, route, seg, cnt8, off, total_rows)
            ys = _moe(hs, blk_e, blk_nv, moe_w_gate[i].astype(BF16), moe_w_up[i].astype(BF16),
                      moe_w_down[i].astype(BF16))
            x = _combine(x.reshape(b * s, d), route, mod, row(final_norm_g), ys, seg, cnt8, off,
                         s).reshape(b, s, d)
    return x
```

```python
import functools

import jax
import jax.numpy as jnp
from jax import lax
from jax.experimental import pallas as pl
from jax.experimental.pallas import tpu as pltpu

EPS = 1e-6
F32 = jnp.float32
BF16 = jnp.bfloat16

SUBLANES = 8
LANES = 128

POOL_WINDOWS = (2, 4, 8, 16)
POOL_GROUP = 64
CONV_KERNEL = 31
HEAD_DIM = 64
CHUNK = 128
N_EXPERTS = 8
HALO = 16

MIX_TM = 512
FFN_TM = 512
FFN_FC = 768
ROUTE_T = 512
MOE_TM = 512
MOE_FSTEP = 1792
MOE_FC = 512
CONV_STRIDE = 4
VMEM_LIMIT = 48 * 1024 * 1024


def _dot(a, b):
    return jnp.dot(a, b, preferred_element_type=F32)


def _dot_tn(a, b):
    return lax.dot_general(a, b, (((0,), (0,)), ((), ())), preferred_element_type=F32)


def _sigmoid(x):
    return 1.0 / (1.0 + jnp.exp(-x))


def _rms_mod(x, g_row, scale_row, shift_row):
    ms = jnp.mean(x * x, axis=-1, keepdims=True)
    return x * lax.rsqrt(ms + EPS) * (g_row * (1.0 + scale_row)) + shift_row


def _layer_norm(x, g_row, b_row):
    mu = jnp.mean(x, axis=-1, keepdims=True)
    xc = x - mu
    var = jnp.mean(xc * xc, axis=-1, keepdims=True)
    return xc * lax.rsqrt(var + EPS) * g_row + b_row


def _mod_kernel(c_ref, w_ref, b_ref, o_ref):
    c = c_ref[...]
    cond = (c * _sigmoid(c)).astype(BF16)
    o_ref[0] = _dot(cond, w_ref[0].astype(BF16)) + b_ref[0]


def _modulation(c, ada_w, ada_b):
    depth, d, nd = ada_w.shape
    b = c.shape[0]
    nblk = nd // d
    return pl.pallas_call(
        _mod_kernel,
        grid=(depth, nblk),
        in_specs=[
            pl.BlockSpec((b, d), lambda l, n: (0, 0)),
            pl.BlockSpec((1, d, d), lambda l, n: (l, 0, n)),
            pl.BlockSpec((1, 1, d), lambda l, n: (l, 0, n)),
        ],
        out_specs=pl.BlockSpec((1, b, d), lambda l, n: (l, 0, n)),
        out_shape=jax.ShapeDtypeStruct((depth, b, nd), F32),
        name="mod",
    )(c, ada_w, ada_b.reshape(depth, 1, nd))


def _mix_kernel(xm_ref, xp_ref, xn_ref, mod_ref, g1_ref, g2_ref, win_ref, pbd_ref, pscale_ref,
                cw_ref, cb_ref, clg_ref, clb_ref, slg_ref, slb_ref, swcat_ref, sbias_ref, wout_ref,
                xo_ref, ho_ref, glu_ref, ystage_ref, mixed_ref, *, tm, seq):
    j = pl.program_id(1)
    nj = pl.num_programs(1)
    ext = tm + 2 * HALO
    pool_w = pbd_ref.shape[0]
    conv_w = cw_ref.shape[1]
    sgu_w = slg_ref.shape[1]
    pc_w = pool_w + 2 * conv_w

    mod = mod_ref[0]
    shift_m, scale_m, gate_m = mod[0:1], mod[1:2], mod[2:3]
    shift_f, scale_f = mod[3:4], mod[4:5]

    xm = xm_ref[0]
    xe = jnp.concatenate([xp_ref[0], xm, xn_ref[0]], axis=0)
    hn = _rms_mod(xe, g1_ref[...], scale_m, shift_m)
    r = lax.broadcasted_iota(jnp.int32, (ext, 1), 0)
    ok = ((r >= HALO) | (j > 0)) & ((r < HALO + tm) | (j < nj - 1))
    hb = jnp.where(ok, hn, 0.0).astype(BF16)

    p1 = _dot(hb, win_ref[:, 0:pc_w])
    p2 = _dot(hb[HALO:HALO + tm], win_ref[:, pc_w:pc_w + 2 * sgu_w])

    a = p1[:, 0:pool_w]
    s2 = a[0:ext - 1] + a[1:ext]
    s4 = s2[0:ext - 3] + s2[2:ext - 1]
    a_hi = s4[:, LANES:pool_w]
    s8 = a_hi[0:ext - 7] + a_hi[4:ext - 3]
    s16 = s8[0:ext - 15] + s8[8:ext - 7]
    lane = lax.broadcasted_iota(jnp.int32, (tm, LANES), 1)
    first = lane < POOL_GROUP
    win_lo = jnp.where(first, s2[HALO - 1:HALO - 1 + tm, 0:LANES], s4[HALO - 2:HALO - 2 + tm, 0:LANES])
    win_hi = jnp.where(first, s8[HALO - 4:HALO - 4 + tm], s16[HALO - 8:HALO - 8 + tm])
    win = jnp.concatenate([win_lo, win_hi], axis=1)
    t = lax.broadcasted_iota(jnp.int32, (tm, pool_w), 0) + j * tm
    lane2 = lax.broadcasted_iota(jnp.int32, (tm, pool_w), 1)
    half = jnp.where(lane2 < POOL_GROUP, 1,
                     jnp.where(lane2 < 2 * POOL_GROUP, 2, jnp.where(lane2 < 3 * POOL_GROUP, 4, 8)))
    count = (jnp.minimum(t + half, seq) - jnp.maximum(t - half, 0)).astype(F32)
    diff = win / count - a[HALO:HALO + tm]
    out_a = _dot(diff.astype(BF16), pbd_ref[...]) * pscale_ref[...]
    mixed_ref[:, 0:pool_w] = out_a.astype(BF16)

    val = p1[:, pool_w:pool_w + conv_w]
    gate = p1[:, pool_w + conv_w:pc_w]
    glu = val * _sigmoid(gate)
    n_slab = conv_w // LANES
    for m in range(n_slab):
        glu_ref[m] = glu[:, m * LANES:(m + 1) * LANES]
    cb = [cb_ref[:, m * LANES:(m + 1) * LANES] for m in range(n_slab)]
    clg = [clg_ref[:, m * LANES:(m + 1) * LANES] for m in range(n_slab)]
    clb = [clb_ref[:, m * LANES:(m + 1) * LANES] for m in range(n_slab)]
    span = SUBLANES * CONV_STRIDE
    for blk in range(tm // span):
        row0 = blk * span
        acc = [[None] * CONV_STRIDE for _ in range(n_slab)]
        for m in range(n_slab):
            for k in range(CONV_KERNEL):
                w = cw_ref[k:k + 1, m * LANES:(m + 1) * LANES]
                for s in range(CONV_STRIDE):
                    start = row0 + s + HALO - CONV_KERNEL // 2 + k
                    term = glu_ref[m, pl.ds(start, SUBLANES, stride=CONV_STRIDE), :] * w
                    acc[m][s] = term if k == 0 else acc[m][s] + term
        for s in range(CONV_STRIDE):
            ys = [acc[m][s] + cb[m] for m in range(n_slab)]
            mu = sum(jnp.sum(y, axis=-1, keepdims=True) for y in ys) * (1.0 / conv_w)
            yc = [y - mu for y in ys]
            var = sum(jnp.sum(y * y, axis=-1, keepdims=True) for y in yc) * (1.0 / conv_w)
            rstd = lax.rsqrt(var + EPS)
            for m in range(n_slab):
                yn = yc[m] * rstd * clg[m] + clb[m]
                ystage_ref[m, pl.ds(row0 + s, SUBLANES, stride=CONV_STRIDE), :] = yn * _sigmoid(yn)
    for m in range(n_slab):
        mixed_ref[:, pool_w + m * LANES:pool_w + (m + 1) * LANES] = ystage_ref[m].astype(BF16)

    u = p2[:, 0:sgu_w]
    v = _layer_norm(p2[:, sgu_w:2 * sgu_w], slg_ref[...], slb_ref[...])
    lane3 = lax.broadcasted_iota(jnp.int32, (tm, sgu_w), 1)
    is_lo = (lane3 % LANES) < HEAD_DIM
    v_lo = jnp.where(is_lo, v, 0.0).astype(BF16)
    v_hi = jnp.where(is_lo, 0.0, v).astype(BF16)
    sbias = sbias_ref[...]
    col0 = pool_w + conv_w
    for cp in range(tm // (2 * CHUNK)):
        ra = 2 * cp * CHUNK
        rb = ra + CHUNK
        for m in range(sgu_w // LANES):
            ls = slice(m * LANES, (m + 1) * LANES)
            rhs = jnp.concatenate([
                jnp.concatenate([v_lo[ra:ra + CHUNK, ls], v_lo[rb:rb + CHUNK, ls]], axis=1),
                jnp.concatenate([v_hi[ra:ra + CHUNK, ls], v_hi[rb:rb + CHUNK, ls]], axis=1)], axis=0)
            mixed = _dot(swcat_ref[m], rhs)
            bias = sbias[:, ls]
            oa = u[ra:ra + CHUNK, ls] * (mixed[:, 0:LANES] + bias)
            ob = u[rb:rb + CHUNK, ls] * (mixed[:, LANES:2 * LANES] + bias)
            mixed_ref[ra:ra + CHUNK, col0 + m * LANES:col0 + (m + 1) * LANES] = oa.astype(BF16)
            mixed_ref[rb:rb + CHUNK, col0 + m * LANES:col0 + (m + 1) * LANES] = ob.astype(BF16)

    xnew = xm + gate_m * _dot(mixed_ref[...], wout_ref[...])
    xo_ref[0] = xnew
    ho_ref[0] = _rms_mod(xnew, g2_ref[...], scale_f, shift_f).astype(BF16)


def _mix_layer(x, mod, g1, g2, w_in, pool_bd, pool_scale, conv_w, conv_b, conv_ln_g, conv_ln_b,
               sgu_ln_g, sgu_ln_b, sgu_wcat, sgu_bias, w_out):
    b, s, d = x.shape
    tm = MIX_TM
    assert s % tm == 0 and tm % (2 * CHUNK) == 0 and tm % HALO == 0
    hb = tm // HALO
    last_halo = s // HALO - 1
    full = lambda arr: pl.BlockSpec(arr.shape, lambda bi, j: (0,) * arr.ndim)
    params = [g1, g2, w_in, pool_bd, pool_scale, conv_w, conv_b, conv_ln_g, conv_ln_b,
              sgu_ln_g, sgu_ln_b, sgu_wcat, sgu_bias, w_out]
    return pl.pallas_call(
        functools.partial(_mix_kernel, tm=tm, seq=s),
        grid=(b, s // tm),
        in_specs=[
            pl.BlockSpec((1, tm, d), lambda bi, j: (bi, j, 0)),
            pl.BlockSpec((1, HALO, d), lambda bi, j: (bi, jnp.maximum(j * hb - 1, 0), 0)),
            pl.BlockSpec((1, HALO, d), lambda bi, j: (bi, jnp.minimum((j + 1) * hb, last_halo), 0)),
            pl.BlockSpec((1,) + mod.shape[1:], lambda bi, j: (bi, 0, 0)),
        ] + [full(p) for p in params],
        out_specs=[
            pl.BlockSpec((1, tm, d), lambda bi, j: (bi, j, 0)),
            pl.BlockSpec((1, tm, d), lambda bi, j: (bi, j, 0)),
        ],
        out_shape=[jax.ShapeDtypeStruct((b, s, d), F32), jax.ShapeDtypeStruct((b, s, d), BF16)],
        scratch_shapes=[
            pltpu.VMEM((conv_w.shape[1] // LANES, tm + 2 * HALO, LANES), F32),
            pltpu.VMEM((conv_w.shape[1] // LANES, tm, LANES), F32),
            pltpu.VMEM((tm, d), BF16),
        ],
        compiler_params=pltpu.CompilerParams(
            dimension_semantics=("arbitrary", "arbitrary"), vmem_limit_bytes=VMEM_LIMIT),
        name="mix",
    )(x, x, x, mod, *params)


def _ffn_kernel(x_ref, h_ref, mod_ref, wg_ref, wu_ref, wd_ref, o_ref, *, fc):
    h = h_ref[0]
    gate_f = mod_ref[0][5:6]
    width = wg_ref.shape[1]
    acc = None
    for lo in range(0, width, fc):
        fs = slice(lo, min(lo + fc, width))
        g = _dot(h, wg_ref[:, fs])
        u = _dot(h, wu_ref[:, fs])
        act = (g * _sigmoid(g) * u).astype(BF16)
        part = _dot(act, wd_ref[fs, :])
        acc = part if acc is None else acc + part
    o_ref[0] = x_ref[0] + gate_f * acc


def _dense_ffn(x, h, mod, wg, wu, wd):
    b, s, d = x.shape
    tm = FFN_TM
    assert s % tm == 0
    full = lambda arr: pl.BlockSpec(arr.shape, lambda bi, j: (0,) * arr.ndim)
    return pl.pallas_call(
        functools.partial(_ffn_kernel, fc=FFN_FC),
        grid=(b, s // tm),
        in_specs=[
            pl.BlockSpec((1, tm, d), lambda bi, j: (bi, j, 0)),
            pl.BlockSpec((1, tm, d), lambda bi, j: (bi, j, 0)),
            pl.BlockSpec((1,) + mod.shape[1:], lambda bi, j: (bi, 0, 0)),
            full(wg), full(wu), full(wd),
        ],
        out_specs=pl.BlockSpec((1, tm, d), lambda bi, j: (bi, j, 0)),
        out_shape=jax.ShapeDtypeStruct((b, s, d), F32),
        compiler_params=pltpu.CompilerParams(
            dimension_semantics=("arbitrary", "arbitrary"), vmem_limit_bytes=VMEM_LIMIT),
        name="ffn",
    )(x, h, mod, wg, wu, wd)


def _route_kernel(h_ref, rw_ref, rb_ref, tri_ref, route_ref, routec_ref, cnt_ref):
    t = h_ref.shape[0]
    logits = _dot(h_ref[...], rw_ref[...])
    lt = logits.T[0:N_EXPERTS] + rb_ref[...]
    eid = lax.broadcasted_iota(jnp.int32, (N_EXPERTS, t), 0).astype(F32)
    m1 = jnp.max(lt, axis=0, keepdims=True)
    i1 = jnp.min(jnp.where(lt == m1, eid, float(N_EXPERTS)), axis=0, keepdims=True)
    sel1 = eid == i1
    lt2 = jnp.where(sel1, -jnp.inf, lt)
    m2 = jnp.max(lt2, axis=0, keepdims=True)
    i2 = jnp.min(jnp.where(lt2 == m2, eid, float(N_EXPERTS)), axis=0, keepdims=True)
    sel2 = eid == i2
    e21 = jnp.exp(m2 - m1)
    w1 = 1.0 / (1.0 + e21)
    w2 = e21 / (1.0 + e21)
    count = jnp.where(sel1 | sel2, 1.0, 0.0)
    rank = _dot(count.astype(BF16), tri_ref[...])
    r1 = jnp.sum(jnp.where(sel1, rank, 0.0), axis=0, keepdims=True)
    r2 = jnp.sum(jnp.where(sel2, rank, 0.0), axis=0, keepdims=True)
    out = jnp.zeros((N_EXPERTS, t), F32)
    for k, rowv in enumerate((i1, i2, w1, w2, r1, r2)):
        out = jnp.where(eid == float(k), rowv, out)
    route_ref[0] = out
    routec_ref[0] = jnp.concatenate([out, jnp.zeros((LANES - N_EXPERTS, t), F32)], axis=0).T
    cnt_ref[0] = jnp.broadcast_to(jnp.sum(count, axis=1, keepdims=True), (N_EXPERTS, LANES))


def _route(h2d, rw_pad, rb_col, tri):
    n, d = h2d.shape
    t = ROUTE_T
    nt = n // t
    return pl.pallas_call(
        _route_kernel,
        grid=(nt,),
        in_specs=[
            pl.BlockSpec((t, d), lambda i: (i, 0)),
            pl.BlockSpec(rw_pad.shape, lambda i: (0, 0)),
            pl.BlockSpec(rb_col.shape, lambda i: (0, 0)),
            pl.BlockSpec(tri.shape, lambda i: (0, 0)),
        ],
        out_specs=[
            pl.BlockSpec((1, N_EXPERTS, t), lambda i: (i, 0, 0)),
            pl.BlockSpec((1, t, LANES), lambda i: (i, 0, 0)),
            pl.BlockSpec((1, N_EXPERTS, LANES), lambda i: (i, 0, 0)),
        ],
        out_shape=[jax.ShapeDtypeStruct((nt, N_EXPERTS, t), F32),
                   jax.ShapeDtypeStruct((nt, t, LANES), F32),
                   jax.ShapeDtypeStruct((nt, N_EXPERTS, LANES), F32)],
        compiler_params=pltpu.CompilerParams(dimension_semantics=("arbitrary",)),
        name="route",
    )(h2d, rw_pad, rb_col, tri)


def _dest_rows(route, seg_ref, i):
    i1, i2, r1, r2 = route[0:1], route[1:2], route[4:5], route[5:6]
    s1 = jnp.zeros_like(i1)
    s2 = jnp.zeros_like(i2)
    for e in range(N_EXPERTS):
        start = seg_ref[i * N_EXPERTS + e].astype(F32)
        s1 = jnp.where(i1 == e, start, s1)
        s2 = jnp.where(i2 == e, start, s2)
    return (s1 + r1).astype(jnp.int32), (s2 + r2).astype(jnp.int32)


def _rows_copy(cnt, vrow, hrow, vmem_ref, hbm_ref, sem, max_rows, to_hbm, wait):
    for k in range((max_rows // SUBLANES).bit_length()):
        size = SUBLANES << k
        done = (cnt >> (k + 4)) << (k + 4)

        @pl.when(((cnt >> (k + 3)) & 1) == 1)
        def _():
            v = vmem_ref.at[pl.ds(pl.multiple_of(vrow + done, SUBLANES), size)]
            g = hbm_ref.at[pl.ds(pl.multiple_of(hrow + done, SUBLANES), size)]
            cp = pltpu.make_async_copy(v, g, sem) if to_hbm else pltpu.make_async_copy(g, v, sem)
            if wait:
                cp.wait()
            else:
                cp.start()


def _segment_copies(i, seg_ref, cnt_ref, off_ref, vmem_ref, hbm_ref, sem, t, to_hbm, wait):
    for e in range(N_EXPERTS):
        k = i * N_EXPERTS + e
        _rows_copy(cnt_ref[k], seg_ref[k], off_ref[k], vmem_ref, hbm_ref, sem, t, to_hbm, wait)


def _dispatch_kernel(seg_ref, cnt_ref, off_ref, pad_ref, h_ref, route_ref, hs_ref, comp_ref, sem, *, tm):
    i = pl.program_id(0)
    last = pl.num_programs(0) - 1
    t = h_ref.shape[0]
    rows = comp_ref.shape[1]
    slot = i % 2
    copies = functools.partial(_segment_copies, seg_ref=seg_ref, cnt_ref=cnt_ref, off_ref=off_ref,
                               hbm_ref=hs_ref, t=t, to_hbm=True)

    @pl.when(i >= 2)
    def _():
        copies(i - 2, vmem_ref=comp_ref.at[slot], sem=sem.at[slot], wait=True)

    d1, d2 = _dest_rows(route_ref[0], seg_ref, i)
    rid = lax.broadcasted_iota(jnp.int32, (rows, t), 0)
    onehot = jnp.where((rid == d1) | (rid == d2), 1.0, 0.0).astype(BF16)
    comp_ref[slot] = _dot(onehot, h_ref[...])
    copies(i, vmem_ref=comp_ref.at[slot], sem=sem.at[slot], wait=False)

    @pl.when(i == last)
    def _():
        @pl.when(i >= 1)
        def _():
            copies(i - 1, vmem_ref=comp_ref.at[1 - slot], sem=sem.at[1 - slot], wait=True)

        copies(i, vmem_ref=comp_ref.at[slot], sem=sem.at[slot], wait=True)
        comp_ref[0, 0:tm, :] = jnp.zeros((tm, comp_ref.shape[2]), F32)
        zeros = comp_ref.at[0]
        for wait in (False, True):
            for e in range(N_EXPERTS):
                _rows_copy(pad_ref[N_EXPERTS + e], 0, pad_ref[e], zeros, hs_ref, sem.at[2], tm // 2, True, wait)

        def tail(g, carry):
            row = pl.multiple_of(g * tm, SUBLANES)
            cp = pltpu.make_async_copy(zeros.at[pl.ds(0, tm)], hs_ref.at[pl.ds(row, tm)], sem.at[2])
            cp.start()
            cp.wait()
            return carry

        lax.fori_loop(pad_ref[2 * N_EXPERTS], pad_ref[2 * N_EXPERTS + 1], tail, 0)


def _dispatch(h2d, route, seg, cnt8, off, pad, total_rows):
    n, d = h2d.shape
    t = ROUTE_T
    nt = n // t
    rows = 2 * t + N_EXPERTS * SUBLANES
    assert rows >= MOE_TM
    return pl.pallas_call(
        functools.partial(_dispatch_kernel, tm=MOE_TM),
        grid_spec=pltpu.PrefetchScalarGridSpec(
            num_scalar_prefetch=4,
            grid=(nt,),
            in_specs=[
                pl.BlockSpec((t, d), lambda i, *_: (i, 0)),
                pl.BlockSpec((1, N_EXPERTS, t), lambda i, *_: (i, 0, 0)),
            ],
            out_specs=pl.BlockSpec(memory_space=pl.ANY),
            scratch_shapes=[pltpu.VMEM((2, rows, d), F32), pltpu.SemaphoreType.DMA((3,))],
        ),
        out_shape=jax.ShapeDtypeStruct((total_rows, d), F32),
        compiler_params=pltpu.CompilerParams(
            dimension_semantics=("arbitrary",), vmem_limit_bytes=VMEM_LIMIT),
        name="dispatch",
    )(seg, cnt8, off, pad, h2d, route)


def _moe_kernel(be_ref, nv_ref, hs_ref, wg_ref, wu_ref, wd_ref, o_ref, hb_ref, *, fc):
    g = pl.program_id(0)
    f = pl.program_id(1)
    nvalid = nv_ref[g]

    @pl.when(nvalid > 0)
    def _():
        @pl.when(f == 0)
        def _():
            hb_ref[...] = hs_ref[...].astype(BF16)

        h = hb_ref[...]
        width = wg_ref.shape[2]
        acc = None
        for lo in range(0, width, fc):
            fs = slice(lo, min(lo + fc, width))
            gt = _dot(h, wg_ref[0, :, fs])
            up = _dot(h, wu_ref[0, :, fs])
            act = (gt * _sigmoid(gt) * up).astype(BF16)
            part = _dot(act, wd_ref[0, fs, :])
            acc = part if acc is None else acc + part

        @pl.when(f == 0)
        def _():
            o_ref[...] = acc

        @pl.when(f > 0)
        def _():
            o_ref[...] += acc

    @pl.when((nvalid == 0) & (f == 0))
    def _():
        o_ref[...] = jnp.zeros_like(o_ref)


def _moe(hs, blk_e, blk_nv, wg, wu, wd):
    rows, d = hs.shape
    tm = MOE_TM
    fstep = MOE_FSTEP
    nf = wg.shape[2] // fstep
    nblk = rows // tm
    assert wg.shape[2] % fstep == 0

    def fidx(g, f, nv):
        return jnp.where(nv[g] > 0, f, nf - 1)

    return pl.pallas_call(
        functools.partial(_moe_kernel, fc=MOE_FC),
        grid_spec=pltpu.PrefetchScalarGridSpec(
            num_scalar_prefetch=2,
            grid=(nblk, nf),
            in_specs=[
                pl.BlockSpec((tm, d), lambda g, f, be, nv: (g, 0)),
                pl.BlockSpec((1, d, fstep), lambda g, f, be, nv: (be[g], 0, fidx(g, f, nv))),
                pl.BlockSpec((1, d, fstep), lambda g, f, be, nv: (be[g], 0, fidx(g, f, nv))),
                pl.BlockSpec((1, fstep, d), lambda g, f, be, nv: (be[g], fidx(g, f, nv), 0)),
            ],
            out_specs=pl.BlockSpec((tm, d), lambda g, f, be, nv: (g, 0)),
            scratch_shapes=[pltpu.VMEM((tm, d), BF16)],
        ),
        out_shape=jax.ShapeDtypeStruct((rows, d), F32),
        compiler_params=pltpu.CompilerParams(
            dimension_semantics=("arbitrary", "arbitrary"), vmem_limit_bytes=VMEM_LIMIT),
        name="moe",
    )(blk_e, blk_nv, hs, wg, wu, wd)


def _combine_kernel(seg_ref, cnt_ref, off_ref, x_ref, routec_ref, mod_ref, gfin_ref, ys_ref, o_ref,
                    comp_ref, sem):
    i = pl.program_id(0)
    t = x_ref.shape[0]
    rows = comp_ref.shape[1]
    slot = i % 2
    fetch = functools.partial(_segment_copies, seg_ref=seg_ref, cnt_ref=cnt_ref, off_ref=off_ref,
                              hbm_ref=ys_ref, t=t, to_hbm=False)

    @pl.when(i == 0)
    def _():
        comp_ref[...] = jnp.zeros_like(comp_ref)
        fetch(i, vmem_ref=comp_ref.at[slot], sem=sem.at[slot], wait=False)

    @pl.when(i + 1 < pl.num_programs(0))
    def _():
        fetch(i + 1, vmem_ref=comp_ref.at[1 - slot], sem=sem.at[1 - slot], wait=False)

    rc = routec_ref[0]
    i1, i2, w1, w2, r1, r2 = (rc[:, k:k + 1] for k in range(6))
    s1 = jnp.zeros_like(i1)
    s2 = jnp.zeros_like(i2)
    for e in range(N_EXPERTS):
        start = seg_ref[i * N_EXPERTS + e].astype(F32)
        s1 = jnp.where(i1 == e, start, s1)
        s2 = jnp.where(i2 == e, start, s2)
    d1 = (s1 + r1).astype(jnp.int32)
    d2 = (s2 + r2).astype(jnp.int32)
    cid = lax.broadcasted_iota(jnp.int32, (t, rows), 1)
    wmat = (jnp.where(cid == d1, w1, 0.0) + jnp.where(cid == d2, w2, 0.0)).astype(BF16)
    fetch(i, vmem_ref=comp_ref.at[slot], sem=sem.at[slot], wait=True)
    y = _dot(wmat, comp_ref[slot].astype(BF16))
    gate_f = mod_ref[0][5:6]
    x2 = x_ref[...] + gate_f * y
    ms = jnp.mean(x2 * x2, axis=-1, keepdims=True)
    o_ref[...] = x2 * lax.rsqrt(ms + EPS) * gfin_ref[...]


def _combine(x2d, routec, mod, gfin, ys, seg, cnt8, off, seq):
    n, d = x2d.shape
    t = ROUTE_T
    nt = n // t
    tiles_per_seq = seq // t
    rows = 2 * t + LANES
    return pl.pallas_call(
        _combine_kernel,
        grid_spec=pltpu.PrefetchScalarGridSpec(
            num_scalar_prefetch=3,
            grid=(nt,),
            in_specs=[
                pl.BlockSpec((t, d), lambda i, *_: (i, 0)),
                pl.BlockSpec((1, t, LANES), lambda i, *_: (i, 0, 0)),
                pl.BlockSpec((1,) + mod.shape[1:], lambda i, *_: (i // tiles_per_seq, 0, 0)),
                pl.BlockSpec(gfin.shape, lambda i, *_: (0, 0)),
                pl.BlockSpec(memory_space=pl.ANY),
            ],
            out_specs=pl.BlockSpec((t, d), lambda i, *_: (i, 0)),
            scratch_shapes=[pltpu.VMEM((2, rows, d), F32), pltpu.SemaphoreType.DMA((2,))],
        ),
        out_shape=jax.ShapeDtypeStruct((n, d), F32),
        compiler_params=pltpu.CompilerParams(
            dimension_semantics=("arbitrary",), vmem_limit_bytes=VMEM_LIMIT),
        name="combine",
    )(seg, cnt8, off, x2d, routec, mod, gfin, ys)


def _moe_plan(cnt, tm):
    nt = cnt.shape[0]
    cnt8 = (cnt + SUBLANES - 1) // SUBLANES * SUBLANES
    seg = jnp.cumsum(cnt8, axis=1) - cnt8
    tot = jnp.sum(cnt8, axis=0)
    nblk = (tot + tm - 1) // tm
    blk_end = jnp.cumsum(nblk)
    blk_start = blk_end - nblk
    off = (blk_start * tm)[None, :] + jnp.cumsum(cnt8, axis=0) - cnt8
    max_rows = 2 * nt * ROUTE_T + nt * N_EXPERTS * (SUBLANES - 1)
    n_blocks = (max_rows + tm - 1) // tm + N_EXPERTS
    g = jnp.arange(n_blocks, dtype=jnp.int32)
    blk_e = jnp.minimum(jnp.searchsorted(blk_end, g, side="right"), N_EXPERTS - 1).astype(jnp.int32)
    blk_nv = jnp.clip(tot[blk_e] - (g - blk_start[blk_e]) * tm, 0, tm).astype(jnp.int32)
    pad = jnp.concatenate([blk_start * tm + tot, nblk * tm - tot, blk_end[-1:],
                           jnp.full((1,), n_blocks, jnp.int32)])
    flat = lambda a: a.reshape(-1).astype(jnp.int32)
    return flat(seg), flat(cnt8), flat(off), flat(pad), blk_e, blk_nv, n_blocks * tm


def kernel(x, c, ada_w, ada_b, mix_norm_g, ffn_norm_g, w_in, pool_w, pool_scale, conv_w, conv_b,
           conv_ln_g, conv_ln_b, sgu_ln_g, sgu_ln_b, sgu_w, sgu_b, w_out, ffn_w_gate, ffn_w_up,
           ffn_w_down, router_w, router_b, moe_w_gate, moe_w_up, moe_w_down, final_norm_g):
    b, s, d = x.shape
    depth = ada_w.shape[0]
    n_mod = ada_w.shape[2] // d
    mod_all = _modulation(c, ada_w, ada_b).reshape(depth, b, n_mod, d)
    row = lambda v: v.reshape(1, -1)

    for l in range(depth):
        mod = mod_all[l]
        pool_bd = jax.scipy.linalg.block_diag(*[pool_w[l, g] for g in range(pool_w.shape[1])]).astype(BF16)
        sw = sgu_w[l]
        sgu_wcat = jnp.concatenate([sw[0::2], sw[1::2]], axis=2).astype(BF16)
        sgu_bias = jnp.repeat(sgu_b[l].T, HEAD_DIM, axis=1)
        x, h = _mix_layer(x, mod, row(mix_norm_g[l]), row(ffn_norm_g[l]), w_in[l].astype(BF16), pool_bd,
                          row(pool_scale[l]), conv_w[l], row(conv_b[l]), row(conv_ln_g[l]),
                          row(conv_ln_b[l]), row(sgu_ln_g[l]), row(sgu_ln_b[l]), sgu_wcat, sgu_bias,
                          w_out[l].astype(BF16))
        i = l // 2
        if l % 2 == 0:
            x = _dense_ffn(x, h, mod, ffn_w_gate[i].astype(BF16), ffn_w_up[i].astype(BF16),
                           ffn_w_down[i].astype(BF16))
        else:
            assert l == depth - 1, "the combine kernel applies the final norm"
            h2d = h.reshape(b * s, d)
            rw_pad = jnp.pad(router_w[i], ((0, 0), (0, LANES - N_EXPERTS))).astype(BF16)
            tri = jnp.triu(jnp.ones((ROUTE_T, ROUTE_T), BF16), k=1)
            route, routec, cnt = _route(h2d, rw_pad, router_b[i].reshape(N_EXPERTS, 1), tri)
            seg, cnt8, off, pad, blk_e, blk_nv, total_rows = _moe_plan(
                cnt[:, :, 0].astype(jnp.int32), MOE_TM)
            hs = _dispatch(h2d, route, seg, cnt8, off, pad, total_rows)
            ys = _moe(hs, blk_e, blk_nv, moe_w_gate[i].astype(BF16), moe_w_up[i].astype(BF16),
                      moe_w_down[i].astype(BF16))
            x = _combine(x.reshape(b * s, d), routec, mod, row(final_norm_g), ys, seg, cnt8, off,
                         s).reshape(b, s, d)
    return x
```

```python
import functools

import jax
import jax.numpy as jnp
from jax import lax
from jax.experimental import pallas as pl
from jax.experimental.pallas import tpu as pltpu

EPS = 1e-6
F32 = jnp.float32
BF16 = jnp.bfloat16

SUBLANES = 8
LANES = 128
MXU_N = 256
_DONE = object()

POOL_WINDOWS = (2, 4, 8, 16)
POOL_GROUP = 64
CONV_KERNEL = 31
HEAD_DIM = 64
CHUNK = 128
N_EXPERTS = 8
HALO = 16

MIX_TM = 512
MIX_SUBTILES = 2
FFN_TM = 512
FFN_FC = 768
ROUTE_T = 512
MOE_TM = 512
MOE_FSTEP = 1792
MOE_FC = 512
CONV_STRIDE = 4
VMEM_LIMIT = 48 * 1024 * 1024


def _dot(a, b):
    return jnp.dot(a, b, preferred_element_type=F32)


def _dot_tn(a, b):
    return lax.dot_general(a, b, (((0,), (0,)), ((), ())), preferred_element_type=F32)


def _sigmoid(x):
    return 1.0 / (1.0 + jnp.exp(-x))


def _rms_mod(x, g_row, scale_row, shift_row):
    ms = jnp.mean(x * x, axis=-1, keepdims=True)
    return x * lax.rsqrt(ms + EPS) * (g_row * (1.0 + scale_row)) + shift_row


def _layer_norm(x, g_row, b_row):
    mu = jnp.mean(x, axis=-1, keepdims=True)
    xc = x - mu
    var = jnp.mean(xc * xc, axis=-1, keepdims=True)
    return xc * lax.rsqrt(var + EPS) * g_row + b_row


def _mod_kernel(c_ref, w_ref, b_ref, o_ref):
    c = c_ref[...]
    cond = (c * _sigmoid(c)).astype(BF16)
    o_ref[0] = _dot(cond, w_ref[0].astype(BF16)) + b_ref[0]


def _modulation(c, ada_w, ada_b):
    depth, d, nd = ada_w.shape
    b = c.shape[0]
    nblk = nd // d
    return pl.pallas_call(
        _mod_kernel,
        grid=(depth, nblk),
        in_specs=[
            pl.BlockSpec((b, d), lambda l, n: (0, 0)),
            pl.BlockSpec((1, d, d), lambda l, n: (l, 0, n)),
            pl.BlockSpec((1, 1, d), lambda l, n: (l, 0, n)),
        ],
        out_specs=pl.BlockSpec((1, b, d), lambda l, n: (l, 0, n)),
        out_shape=jax.ShapeDtypeStruct((depth, b, nd), F32),
        name="mod",
    )(c, ada_w, ada_b.reshape(depth, 1, nd))


def _route_tables(h, rw, rb, tri):
    t = h.shape[0]
    logits = _dot(h, rw)
    lt = logits.T[0:N_EXPERTS] + rb
    eid = lax.broadcasted_iota(jnp.int32, (N_EXPERTS, t), 0).astype(F32)
    m1 = jnp.max(lt, axis=0, keepdims=True)
    i1 = jnp.min(jnp.where(lt == m1, eid, float(N_EXPERTS)), axis=0, keepdims=True)
    sel1 = eid == i1
    lt2 = jnp.where(sel1, -jnp.inf, lt)
    m2 = jnp.max(lt2, axis=0, keepdims=True)
    i2 = jnp.min(jnp.where(lt2 == m2, eid, float(N_EXPERTS)), axis=0, keepdims=True)
    sel2 = eid == i2
    e21 = jnp.exp(m2 - m1)
    w1 = 1.0 / (1.0 + e21)
    w2 = e21 / (1.0 + e21)
    count = jnp.where(sel1 | sel2, 1.0, 0.0)
    rank = _dot(count.astype(BF16), tri)
    r1 = jnp.sum(jnp.where(sel1, rank, 0.0), axis=0, keepdims=True)
    r2 = jnp.sum(jnp.where(sel2, rank, 0.0), axis=0, keepdims=True)
    out = jnp.zeros((N_EXPERTS, t), F32)
    for k, rowv in enumerate((i1, i2, w1, w2, r1, r2)):
        out = jnp.where(eid == float(k), rowv, out)
    return out, jnp.sum(count, axis=1, keepdims=True)


def _mix_kernel(xm_ref, xp_ref, xn_ref, mod_ref, g1_ref, g2_ref, win_ref, pbd_ref, pscale_ref,
                cw_ref, cb_ref, clg_ref, clb_ref, slg_ref, slb_ref, swcat_ref, sbias_ref, wout_ref,
                *rest, tm, seq, n_sub, with_route):
    if with_route:
        rw_ref, rb_ref, tri_ref, xo_ref, ho_ref, route_ref, routec_ref, cnt_ref = rest[:8]
    else:
        xo_ref, ho_ref = rest[:2]
    glu_ref, ystage_ref, mixed_ref = rest[-3:]
    j = pl.program_id(1)
    nj = pl.num_programs(1)
    st = tm // n_sub
    ext = st + 2 * HALO
    pool_w = pbd_ref.shape[0]
    conv_w = cw_ref.shape[1]
    sgu_w = slg_ref.shape[1]
    pc_w = pool_w + 2 * conv_w
    n_slab = conv_w // LANES
    span = SUBLANES * CONV_STRIDE

    mod = mod_ref[0]
    shift_m, scale_m, gate_m = mod[0:1], mod[1:2], mod[2:3]
    shift_f, scale_f = mod[3:4], mod[4:5]
    cb = [cb_ref[:, m * LANES:(m + 1) * LANES] for m in range(n_slab)]
    clg = [clg_ref[:, m * LANES:(m + 1) * LANES] for m in range(n_slab)]
    clb = [clb_ref[:, m * LANES:(m + 1) * LANES] for m in range(n_slab)]
    lane = lax.broadcasted_iota(jnp.int32, (1, pool_w), 1)
    half = jnp.where(lane < POOL_GROUP, 1,
                     jnp.where(lane < 2 * POOL_GROUP, 2, jnp.where(lane < 3 * POOL_GROUP, 4, 8)))
    inv_full = 1.0 / (2 * half).astype(F32)

    def edge_inv(t0):
        t = lax.broadcasted_iota(jnp.int32, (SUBLANES, pool_w), 0) + t0
        return 1.0 / (jnp.minimum(t + half, seq) - jnp.maximum(t - half, 0)).astype(F32)

    state = {}

    def project(sub):
        r0 = sub * st
        first, last = sub == 0, sub == n_sub - 1
        xm = xm_ref[0, r0:r0 + st, :]
        xprev = xp_ref[0] if first else xm_ref[0, r0 - HALO:r0, :]
        xnext = xn_ref[0] if last else xm_ref[0, r0 + st:r0 + st + HALO, :]
        xe = jnp.concatenate([xprev, xm, xnext], axis=0)
        hn = _rms_mod(xe, g1_ref[...], scale_m, shift_m)
        if first or last:
            r = lax.broadcasted_iota(jnp.int32, (ext, 1), 0)
            ok = None
            if first:
                ok = (r >= HALO) | (j > 0)
            if last:
                ok_n = (r < HALO + st) | (j < nj - 1)
                ok = ok_n if ok is None else ok & ok_n
            hn = jnp.where(ok, hn, 0.0)
        hb = hn.astype(BF16)

        cols = []
        for c in range(0, pc_w, MXU_N):
            cols.append(_dot(hb, win_ref[:, c:c + MXU_N]))
            yield
        p1 = jnp.concatenate(cols, axis=1)
        hbm = hb[HALO:HALO + st]
        cols = []
        for c in range(pc_w, pc_w + 2 * sgu_w, MXU_N):
            cols.append(_dot(hbm, win_ref[:, c:c + MXU_N]))
            yield
        p2 = jnp.concatenate(cols, axis=1)
        val = p1[:, pool_w:pool_w + conv_w]
        gate = p1[:, pool_w + conv_w:pc_w]
        glu = val * _sigmoid(gate)
        for m in range(n_slab):
            glu_ref[sub, m] = glu[:, m * LANES:(m + 1) * LANES]
        state[sub] = (xm, p1[:, 0:pool_w], p2)

    def mixers(sub):
        r0 = sub * st
        first, last = sub == 0, sub == n_sub - 1
        xm, a, p2 = state[sub]

        s2 = a[0:ext - 1] + a[1:ext]
        s4 = s2[0:ext - 3] + s2[2:ext - 1]
        a_hi = s4[:, LANES:pool_w]
        s8 = a_hi[0:ext - 7] + a_hi[4:ext - 3]
        s16 = s8[0:ext - 15] + s8[8:ext - 7]
        is_first_group = lax.broadcasted_iota(jnp.int32, (st, LANES), 1) < POOL_GROUP
        win_lo = jnp.where(is_first_group, s2[HALO - 1:HALO - 1 + st, 0:LANES],
                           s4[HALO - 2:HALO - 2 + st, 0:LANES])
        win_hi = jnp.where(is_first_group, s8[HALO - 4:HALO - 4 + st], s16[HALO - 8:HALO - 8 + st])
        win = jnp.concatenate([win_lo, win_hi], axis=1)
        am = a[HALO:HALO + st]
        lo_rows = SUBLANES if first else 0
        hi_rows = SUBLANES if last else 0
        pieces = []
        if first:
            pieces.append(win[0:SUBLANES] * edge_inv(j * tm + r0) - am[0:SUBLANES])
        pieces.append(win[lo_rows:st - hi_rows] * inv_full - am[lo_rows:st - hi_rows])
        if last:
            pieces.append(win[st - SUBLANES:st] * edge_inv(j * tm + r0 + st - SUBLANES) - am[st - SUBLANES:st])
        diff = jnp.concatenate(pieces, axis=0) if len(pieces) > 1 else pieces[0]
        out_a = _dot(diff.astype(BF16), pbd_ref[...]) * pscale_ref[...]
        mixed_ref[r0:r0 + st, 0:pool_w] = out_a.astype(BF16)
        yield

        for blk in range(st // span):
            row0 = blk * span
            acc = [[None] * CONV_STRIDE for _ in range(n_slab)]
            for m in range(n_slab):
                w = {}
                for a in range(CONV_KERNEL + CONV_STRIDE - 1):
                    if a < CONV_KERNEL:
                        w[a] = cw_ref[a:a + 1, m * LANES:(m + 1) * LANES]
                    start = row0 + HALO - CONV_KERNEL // 2 + a
                    rows = glu_ref[sub, m, pl.ds(start, SUBLANES, stride=CONV_STRIDE), :]
                    for s in range(CONV_STRIDE):
                        k = a - s
                        if 0 <= k < CONV_KERNEL:
                            term = rows * w[k]
                            acc[m][s] = term if k == 0 else acc[m][s] + term
            for s in range(CONV_STRIDE):
                ys = [acc[m][s] + cb[m] for m in range(n_slab)]
                mu = sum(jnp.sum(y, axis=-1, keepdims=True) for y in ys) * (1.0 / conv_w)
                yc = [y - mu for y in ys]
                var = sum(jnp.sum(y * y, axis=-1, keepdims=True) for y in yc) * (1.0 / conv_w)
                rstd = lax.rsqrt(var + EPS)
                for m in range(n_slab):
                    yn = yc[m] * rstd * clg[m] + clb[m]
                    ystage_ref[sub, m, pl.ds(row0 + s, SUBLANES, stride=CONV_STRIDE), :] = yn * _sigmoid(yn)
            yield
        for m in range(n_slab):
            mixed_ref[r0:r0 + st, pool_w + m * LANES:pool_w + (m + 1) * LANES] = ystage_ref[sub, m].astype(BF16)

        u = p2[:, 0:sgu_w]
        v = _layer_norm(p2[:, sgu_w:2 * sgu_w], slg_ref[...], slb_ref[...])
        is_lo = (lax.broadcasted_iota(jnp.int32, (st, sgu_w), 1) % LANES) < HEAD_DIM
        v_lo = jnp.where(is_lo, v, 0.0).astype(BF16)
        v_hi = jnp.where(is_lo, 0.0, v).astype(BF16)
        sbias = sbias_ref[...]
        col0 = pool_w + conv_w
        for cp in range(st // (2 * CHUNK)):
            ra = 2 * cp * CHUNK
            rb = ra + CHUNK
            for m in range(sgu_w // LANES):
                ls = slice(m * LANES, (m + 1) * LANES)
                rhs = jnp.concatenate([
                    jnp.concatenate([v_lo[ra:ra + CHUNK, ls], v_lo[rb:rb + CHUNK, ls]], axis=1),
                    jnp.concatenate([v_hi[ra:ra + CHUNK, ls], v_hi[rb:rb + CHUNK, ls]], axis=1)], axis=0)
                mixed = _dot(swcat_ref[m], rhs)
                bias = sbias[:, ls]
                oa = u[ra:ra + CHUNK, ls] * (mixed[:, 0:LANES] + bias)
                ob = u[rb:rb + CHUNK, ls] * (mixed[:, LANES:2 * LANES] + bias)
                cs = slice(col0 + m * LANES, col0 + (m + 1) * LANES)
                mixed_ref[r0 + ra:r0 + ra + CHUNK, cs] = oa.astype(BF16)
                mixed_ref[r0 + rb:r0 + rb + CHUNK, cs] = ob.astype(BF16)
                yield

    def project_out(sub):
        r0 = sub * st
        xm = state[sub][0]
        mixed = mixed_ref[r0:r0 + st, :]
        cols = []
        for c in range(0, wout_ref.shape[1], MXU_N):
            res = _dot(mixed, wout_ref[:, c:c + MXU_N])
            cols.append(xm[:, c:c + MXU_N] + gate_m[:, c:c + MXU_N] * res)
            yield
        xnew = jnp.concatenate(cols, axis=1)
        xo_ref[0, r0:r0 + st, :] = xnew
        ho_ref[0, r0:r0 + st, :] = _rms_mod(xnew, g2_ref[...], scale_f, shift_f).astype(BF16)

    def emit(*stages):
        live = list(stages)
        while live:
            for gen in list(live):
                if next(gen, _DONE) is _DONE:
                    live.remove(gen)

    for step in range(n_sub + 2):
        emit(*([project_out(step - 2)] if 0 <= step - 2 < n_sub else []),
             *([mixers(step - 1)] if 0 <= step - 1 < n_sub else []),
             *([project(step)] if step < n_sub else []))

    if with_route:
        table, cnt = _route_tables(ho_ref[0], rw_ref[...], rb_ref[...], tri_ref[...])
        route_ref[0] = table
        routec_ref[0] = jnp.concatenate([table, jnp.zeros((LANES - N_EXPERTS, tm), F32)], axis=0).T
        cnt_ref[0] = jnp.broadcast_to(cnt, (N_EXPERTS, LANES))


def _mix_layer(x, mod, g1, g2, w_in, pool_bd, pool_scale, conv_w, conv_b, conv_ln_g, conv_ln_b,
               sgu_ln_g, sgu_ln_b, sgu_wcat, sgu_bias, w_out, router=None):
    b, s, d = x.shape
    tm = MIX_TM
    n_sub = MIX_SUBTILES
    st = tm // n_sub
    assert s % tm == 0 and tm % n_sub == 0 and st % (2 * CHUNK) == 0 and st % (SUBLANES * CONV_STRIDE) == 0
    n_slab = conv_w.shape[1] // LANES
    hb = tm // HALO
    nj = s // tm
    last_halo = s // HALO - 1
    full = lambda arr: pl.BlockSpec(arr.shape, lambda bi, j: (0,) * arr.ndim)
    params = [g1, g2, w_in, pool_bd, pool_scale, conv_w, conv_b, conv_ln_g, conv_ln_b,
              sgu_ln_g, sgu_ln_b, sgu_wcat, sgu_bias, w_out]
    out_specs = [
        pl.BlockSpec((1, tm, d), lambda bi, j: (bi, j, 0)),
        pl.BlockSpec((1, tm, d), lambda bi, j: (bi, j, 0)),
    ]
    out_shape = [jax.ShapeDtypeStruct((b, s, d), F32), jax.ShapeDtypeStruct((b, s, d), BF16)]
    if router is not None:
        assert tm == ROUTE_T
        params += list(router)
        tile = lambda bi, j: (bi * nj + j, 0, 0)
        out_specs += [pl.BlockSpec((1, N_EXPERTS, tm), tile), pl.BlockSpec((1, tm, LANES), tile),
                      pl.BlockSpec((1, N_EXPERTS, LANES), tile)]
        out_shape += [jax.ShapeDtypeStruct((b * nj, N_EXPERTS, tm), F32),
                      jax.ShapeDtypeStruct((b * nj, tm, LANES), F32),
                      jax.ShapeDtypeStruct((b * nj, N_EXPERTS, LANES), F32)]
    return pl.pallas_call(
        functools.partial(_mix_kernel, tm=tm, seq=s, n_sub=n_sub, with_route=router is not None),
        grid=(b, nj),
        in_specs=[
            pl.BlockSpec((1, tm, d), lambda bi, j: (bi, j, 0)),
            pl.BlockSpec((1, HALO, d), lambda bi, j: (bi, jnp.maximum(j * hb - 1, 0), 0)),
            pl.BlockSpec((1, HALO, d), lambda bi, j: (bi, jnp.minimum((j + 1) * hb, last_halo), 0)),
            pl.BlockSpec((1,) + mod.shape[1:], lambda bi, j: (bi, 0, 0)),
        ] + [full(p) for p in params],
        out_specs=out_specs,
        out_shape=out_shape,
        scratch_shapes=[
            pltpu.VMEM((n_sub, n_slab, st + 2 * HALO, LANES), F32),
            pltpu.VMEM((n_sub, n_slab, st, LANES), F32),
            pltpu.VMEM((tm, d), BF16),
        ],
        compiler_params=pltpu.CompilerParams(
            dimension_semantics=("arbitrary", "arbitrary"), vmem_limit_bytes=VMEM_LIMIT),
        name="mix",
    )(x, x, x, mod, *params)


def _ffn_kernel(x_ref, h_ref, mod_ref, wg_ref, wu_ref, wd_ref, o_ref, *, fc):
    h = h_ref[0]
    gate_f = mod_ref[0][5:6]
    width = wg_ref.shape[1]
    acc = None
    for lo in range(0, width, fc):
        fs = slice(lo, min(lo + fc, width))
        g = _dot(h, wg_ref[:, fs])
        u = _dot(h, wu_ref[:, fs])
        act = (g * _sigmoid(g) * u).astype(BF16)
        part = _dot(act, wd_ref[fs, :])
        acc = part if acc is None else acc + part
    o_ref[0] = x_ref[0] + gate_f * acc


def _dense_ffn(x, h, mod, wg, wu, wd):
    b, s, d = x.shape
    tm = FFN_TM
    assert s % tm == 0
    full = lambda arr: pl.BlockSpec(arr.shape, lambda bi, j: (0,) * arr.ndim)
    return pl.pallas_call(
        functools.partial(_ffn_kernel, fc=FFN_FC),
        grid=(b, s // tm),
        in_specs=[
            pl.BlockSpec((1, tm, d), lambda bi, j: (bi, j, 0)),
            pl.BlockSpec((1, tm, d), lambda bi, j: (bi, j, 0)),
            pl.BlockSpec((1,) + mod.shape[1:], lambda bi, j: (bi, 0, 0)),
            full(wg), full(wu), full(wd),
        ],
        out_specs=pl.BlockSpec((1, tm, d), lambda bi, j: (bi, j, 0)),
        out_shape=jax.ShapeDtypeStruct((b, s, d), F32),
        compiler_params=pltpu.CompilerParams(
            dimension_semantics=("arbitrary", "arbitrary"), vmem_limit_bytes=VMEM_LIMIT),
        name="ffn",
    )(x, h, mod, wg, wu, wd)


U32 = jnp.uint32
_HI_MASK = 0xFFFF0000


def _pack_bf16_pairs(x):
    half = x.shape[1] // 2
    lo = lax.bitcast_convert_type(x[:, :half], U32) >> 16
    hi = lax.bitcast_convert_type(x[:, half:], U32) & U32(_HI_MASK)
    return lo | hi


def _unpack_bf16_pairs(w):
    lo = lax.bitcast_convert_type(w << 16, F32).astype(BF16)
    hi = lax.bitcast_convert_type(w & U32(_HI_MASK), F32).astype(BF16)
    return jnp.concatenate([lo, hi], axis=1)


def _dest_rows(route, seg_ref, i):
    i1, i2, r1, r2 = route[0:1], route[1:2], route[4:5], route[5:6]
    s1 = jnp.zeros_like(i1)
    s2 = jnp.zeros_like(i2)
    for e in range(N_EXPERTS):
        start = seg_ref[i * N_EXPERTS + e].astype(F32)
        s1 = jnp.where(i1 == e, start, s1)
        s2 = jnp.where(i2 == e, start, s2)
    return (s1 + r1).astype(jnp.int32), (s2 + r2).astype(jnp.int32)


def _rows_copy(cnt, vrow, hrow, vmem_ref, hbm_ref, sem, max_rows, to_hbm, wait):
    for k in range((max_rows // SUBLANES).bit_length()):
        size = SUBLANES << k
        done = (cnt >> (k + 4)) << (k + 4)

        @pl.when(((cnt >> (k + 3)) & 1) == 1)
        def _():
            v = vmem_ref.at[pl.ds(pl.multiple_of(vrow + done, SUBLANES), size)]
            g = hbm_ref.at[pl.ds(pl.multiple_of(hrow + done, SUBLANES), size)]
            cp = pltpu.make_async_copy(v, g, sem) if to_hbm else pltpu.make_async_copy(g, v, sem)
            if wait:
                cp.wait()
            else:
                cp.start()


def _segment_copies(i, seg_ref, cnt_ref, off_ref, vmem_ref, hbm_ref, sem, t, to_hbm, wait):
    for e in range(N_EXPERTS):
        k = i * N_EXPERTS + e
        _rows_copy(cnt_ref[k], seg_ref[k], off_ref[k], vmem_ref, hbm_ref, sem, t, to_hbm, wait)


def _dispatch_kernel(seg_ref, cnt_ref, off_ref, pad_ref, h_ref, route_ref, hs_ref, comp_ref, sem, *, tm):
    i = pl.program_id(0)
    last = pl.num_programs(0) - 1
    t = h_ref.shape[0]
    rows = comp_ref.shape[1]
    slot = i % 2
    copies = functools.partial(_segment_copies, seg_ref=seg_ref, cnt_ref=cnt_ref, off_ref=off_ref,
                               hbm_ref=hs_ref, t=t, to_hbm=True)

    @pl.when(i >= 2)
    def _():
        copies(i - 2, vmem_ref=comp_ref.at[slot], sem=sem.at[slot], wait=True)

    d1, d2 = _dest_rows(route_ref[0], seg_ref, i)
    rid = lax.broadcasted_iota(jnp.int32, (rows, t), 0)
    onehot = jnp.where((rid == d1) | (rid == d2), 1.0, 0.0).astype(BF16)
    comp_ref[slot] = _pack_bf16_pairs(_dot(onehot, h_ref[...]))
    copies(i, vmem_ref=comp_ref.at[slot], sem=sem.at[slot], wait=False)

    @pl.when(i == last)
    def _():
        @pl.when(i >= 1)
        def _():
            copies(i - 1, vmem_ref=comp_ref.at[1 - slot], sem=sem.at[1 - slot], wait=True)

        copies(i, vmem_ref=comp_ref.at[slot], sem=sem.at[slot], wait=True)
        comp_ref[0, 0:tm, :] = jnp.zeros((tm, comp_ref.shape[2]), U32)
        zeros = comp_ref.at[0]
        for wait in (False, True):
            for e in range(N_EXPERTS):
                _rows_copy(pad_ref[N_EXPERTS + e], 0, pad_ref[e], zeros, hs_ref, sem.at[2], tm // 2, True, wait)

        def tail(g, carry):
            row = pl.multiple_of(g * tm, SUBLANES)
            cp = pltpu.make_async_copy(zeros.at[pl.ds(0, tm)], hs_ref.at[pl.ds(row, tm)], sem.at[2])
            cp.start()
            cp.wait()
            return carry

        lax.fori_loop(pad_ref[2 * N_EXPERTS], pad_ref[2 * N_EXPERTS + 1], tail, 0)


def _dispatch(h2d, route, seg, cnt8, off, pad, total_rows):
    n, d = h2d.shape
    t = ROUTE_T
    nt = n // t
    rows = 2 * t + N_EXPERTS * SUBLANES
    assert rows >= MOE_TM
    return pl.pallas_call(
        functools.partial(_dispatch_kernel, tm=MOE_TM),
        grid_spec=pltpu.PrefetchScalarGridSpec(
            num_scalar_prefetch=4,
            grid=(nt,),
            in_specs=[
                pl.BlockSpec((t, d), lambda i, *_: (i, 0)),
                pl.BlockSpec((1, N_EXPERTS, t), lambda i, *_: (i, 0, 0)),
            ],
            out_specs=pl.BlockSpec(memory_space=pl.ANY),
            scratch_shapes=[pltpu.VMEM((2, rows, d // 2), U32), pltpu.SemaphoreType.DMA((3,))],
        ),
        out_shape=jax.ShapeDtypeStruct((total_rows, d // 2), U32),
        compiler_params=pltpu.CompilerParams(
            dimension_semantics=("arbitrary",), vmem_limit_bytes=VMEM_LIMIT),
        name="dispatch",
    )(seg, cnt8, off, pad, h2d, route)


def _moe_kernel(be_ref, nv_ref, hs_ref, wg_ref, wu_ref, wd_ref, o_ref, hb_ref, acc_ref, *, fc):
    g = pl.program_id(0)
    f = pl.program_id(1)
    nf = pl.num_programs(1)
    nvalid = nv_ref[g]

    @pl.when(nvalid > 0)
    def _():
        @pl.when(f == 0)
        def _():
            hb_ref[...] = _unpack_bf16_pairs(hs_ref[...])

        h = hb_ref[...]
        width = wg_ref.shape[2]
        acc = None
        for lo in range(0, width, fc):
            fs = slice(lo, min(lo + fc, width))
            gt = _dot(h, wg_ref[0, :, fs])
            up = _dot(h, wu_ref[0, :, fs])
            act = (gt * _sigmoid(gt) * up).astype(BF16)
            part = _dot(act, wd_ref[0, fs, :])
            acc = part if acc is None else acc + part

        @pl.when(f == 0)
        def _():
            acc_ref[...] = acc

        @pl.when((f > 0) & (f < nf - 1))
        def _():
            acc_ref[...] += acc

        @pl.when(f == nf - 1)
        def _():
            total = acc_ref[...] + acc
            o_ref[...] = _pack_bf16_pairs(total.astype(BF16).astype(F32))

    @pl.when((nvalid == 0) & (f == 0))
    def _():
        o_ref[...] = jnp.zeros_like(o_ref)


def _moe(hs, blk_e, blk_nv, wg, wu, wd):
    rows = hs.shape[0]
    d = wg.shape[1]
    tm = MOE_TM
    fstep = MOE_FSTEP
    nf = wg.shape[2] // fstep
    nblk = rows // tm
    assert wg.shape[2] % fstep == 0 and nf >= 2

    def fidx(g, f, nv):
        return jnp.where(nv[g] > 0, f, nf - 1)

    return pl.pallas_call(
        functools.partial(_moe_kernel, fc=MOE_FC),
        grid_spec=pltpu.PrefetchScalarGridSpec(
            num_scalar_prefetch=2,
            grid=(nblk, nf),
            in_specs=[
                pl.BlockSpec((tm, d // 2), lambda g, f, be, nv: (g, 0)),
                pl.BlockSpec((1, d, fstep), lambda g, f, be, nv: (be[g], 0, fidx(g, f, nv))),
                pl.BlockSpec((1, d, fstep), lambda g, f, be, nv: (be[g], 0, fidx(g, f, nv))),
                pl.BlockSpec((1, fstep, d), lambda g, f, be, nv: (be[g], fidx(g, f, nv), 0)),
            ],
            out_specs=pl.BlockSpec((tm, d // 2), lambda g, f, be, nv: (g, 0)),
            scratch_shapes=[pltpu.VMEM((tm, d), BF16), pltpu.VMEM((tm, d), F32)],
        ),
        out_shape=jax.ShapeDtypeStruct((rows, d // 2), U32),
        compiler_params=pltpu.CompilerParams(
            dimension_semantics=("arbitrary", "arbitrary"), vmem_limit_bytes=VMEM_LIMIT),
        name="moe",
    )(blk_e, blk_nv, hs, wg, wu, wd)


def _combine_kernel(seg_ref, cnt_ref, off_ref, x_ref, routec_ref, mod_ref, gfin_ref, ys_ref, o_ref,
                    comp_ref, sem):
    i = pl.program_id(0)
    t = x_ref.shape[0]
    rows = comp_ref.shape[1]
    slot = i % 2
    fetch = functools.partial(_segment_copies, seg_ref=seg_ref, cnt_ref=cnt_ref, off_ref=off_ref,
                              hbm_ref=ys_ref, t=t, to_hbm=False)

    @pl.when(i == 0)
    def _():
        comp_ref[...] = jnp.zeros_like(comp_ref)
        fetch(i, vmem_ref=comp_ref.at[slot], sem=sem.at[slot], wait=False)

    @pl.when(i + 1 < pl.num_programs(0))
    def _():
        fetch(i + 1, vmem_ref=comp_ref.at[1 - slot], sem=sem.at[1 - slot], wait=False)

    rc = routec_ref[0]
    i1, i2, w1, w2, r1, r2 = (rc[:, k:k + 1] for k in range(6))
    s1 = jnp.zeros_like(i1)
    s2 = jnp.zeros_like(i2)
    for e in range(N_EXPERTS):
        start = seg_ref[i * N_EXPERTS + e].astype(F32)
        s1 = jnp.where(i1 == e, start, s1)
        s2 = jnp.where(i2 == e, start, s2)
    d1 = (s1 + r1).astype(jnp.int32)
    d2 = (s2 + r2).astype(jnp.int32)
    cid = lax.broadcasted_iota(jnp.int32, (t, rows), 1)
    wmat = (jnp.where(cid == d1, w1, 0.0) + jnp.where(cid == d2, w2, 0.0)).astype(BF16)
    fetch(i, vmem_ref=comp_ref.at[slot], sem=sem.at[slot], wait=True)
    y = _dot(wmat, _unpack_bf16_pairs(comp_ref[slot]))
    gate_f = mod_ref[0][5:6]
    x2 = x_ref[...] + gate_f * y
    ms = jnp.mean(x2 * x2, axis=-1, keepdims=True)
    o_ref[...] = x2 * lax.rsqrt(ms + EPS) * gfin_ref[...]


def _combine(x2d, routec, mod, gfin, ys, seg, cnt8, off, seq):
    n, d = x2d.shape
    t = ROUTE_T
    nt = n // t
    tiles_per_seq = seq // t
    rows = 2 * t + LANES
    return pl.pallas_call(
        _combine_kernel,
        grid_spec=pltpu.PrefetchScalarGridSpec(
            num_scalar_prefetch=3,
            grid=(nt,),
            in_specs=[
                pl.BlockSpec((t, d), lambda i, *_: (i, 0)),
                pl.BlockSpec((1, t, LANES), lambda i, *_: (i, 0, 0)),
                pl.BlockSpec((1,) + mod.shape[1:], lambda i, *_: (i // tiles_per_seq, 0, 0)),
                pl.BlockSpec(gfin.shape, lambda i, *_: (0, 0)),
                pl.BlockSpec(memory_space=pl.ANY),
            ],
            out_specs=pl.BlockSpec((t, d), lambda i, *_: (i, 0)),
            scratch_shapes=[pltpu.VMEM((2, rows, d // 2), U32), pltpu.SemaphoreType.DMA((2,))],
        ),
        out_shape=jax.ShapeDtypeStruct((n, d), F32),
        compiler_params=pltpu.CompilerParams(
            dimension_semantics=("arbitrary",), vmem_limit_bytes=VMEM_LIMIT),
        name="combine",
    )(seg, cnt8, off, x2d, routec, mod, gfin, ys)


def _moe_plan(cnt, tm):
    nt = cnt.shape[0]
    cnt8 = (cnt + SUBLANES - 1) // SUBLANES * SUBLANES
    seg = jnp.cumsum(cnt8, axis=1) - cnt8
    tot = jnp.sum(cnt8, axis=0)
    nblk = (tot + tm - 1) // tm
    blk_end = jnp.cumsum(nblk)
    blk_start = blk_end - nblk
    off = (blk_start * tm)[None, :] + jnp.cumsum(cnt8, axis=0) - cnt8
    max_rows = 2 * nt * ROUTE_T + nt * N_EXPERTS * (SUBLANES - 1)
    n_blocks = (max_rows + tm - 1) // tm + N_EXPERTS
    g = jnp.arange(n_blocks, dtype=jnp.int32)
    blk_e = jnp.minimum(jnp.sum(g[:, None] >= blk_end[None, :], axis=1), N_EXPERTS - 1).astype(jnp.int32)
    blk_nv = jnp.clip(tot[blk_e] - (g - blk_start[blk_e]) * tm, 0, tm).astype(jnp.int32)
    pad = jnp.concatenate([blk_start * tm + tot, nblk * tm - tot, blk_end[-1:],
                           jnp.full((1,), n_blocks, jnp.int32)])
    flat = lambda a: a.reshape(-1).astype(jnp.int32)
    return flat(seg), flat(cnt8), flat(off), flat(pad), blk_e, blk_nv, n_blocks * tm


def kernel(x, c, ada_w, ada_b, mix_norm_g, ffn_norm_g, w_in, pool_w, pool_scale, conv_w, conv_b,
           conv_ln_g, conv_ln_b, sgu_ln_g, sgu_ln_b, sgu_w, sgu_b, w_out, ffn_w_gate, ffn_w_up,
           ffn_w_down, router_w, router_b, moe_w_gate, moe_w_up, moe_w_down, final_norm_g):
    b, s, d = x.shape
    depth = ada_w.shape[0]
    n_mod = ada_w.shape[2] // d
    mod_all = _modulation(c, ada_w, ada_b).reshape(depth, b, n_mod, d)
    row = lambda v: v.reshape(1, -1)

    for l in range(depth):
        mod = mod_all[l]
        pool_bd = jax.scipy.linalg.block_diag(*[pool_w[l, g] for g in range(pool_w.shape[1])]).astype(BF16)
        sw = sgu_w[l]
        sgu_wcat = jnp.concatenate([sw[0::2], sw[1::2]], axis=2).astype(BF16)
        sgu_bias = jnp.repeat(sgu_b[l].T, HEAD_DIM, axis=1)
        i = l // 2
        router = None
        if l % 2 == 1:
            rw_pad = jnp.pad(router_w[i], ((0, 0), (0, LANES - N_EXPERTS))).astype(BF16)
            tri = jnp.triu(jnp.ones((ROUTE_T, ROUTE_T), BF16), k=1)
            router = (rw_pad, router_b[i].reshape(N_EXPERTS, 1), tri)
        x, h, *tables = _mix_layer(x, mod, row(mix_norm_g[l]), row(ffn_norm_g[l]), w_in[l].astype(BF16),
                                   pool_bd, row(pool_scale[l]), conv_w[l], row(conv_b[l]),
                                   row(conv_ln_g[l]), row(conv_ln_b[l]), row(sgu_ln_g[l]),
                                   row(sgu_ln_b[l]), sgu_wcat, sgu_bias, w_out[l].astype(BF16), router)
        if l % 2 == 0:
            x = _dense_ffn(x, h, mod, ffn_w_gate[i].astype(BF16), ffn_w_up[i].astype(BF16),
                           ffn_w_down[i].astype(BF16))
        else:
            assert l == depth - 1, "the combine kernel applies the final norm"
            h2d = h.reshape(b * s, d)
            route, routec, cnt = tables
            seg, cnt8, off, pad, blk_e, blk_nv, total_rows = _moe_plan(
                cnt[:, :, 0].astype(jnp.int32), MOE_TM)
            hs = _dispatch(h2d, route, seg, cnt8, off, pad, total_rows)
            ys = _moe(hs, blk_e, blk_nv, moe_w_gate[i].astype(BF16), moe_w_up[i].astype(BF16),
                      moe_w_down[i].astype(BF16))
            x = _combine(x.reshape(b * s, d), routec, mod, row(final_norm_g), ys, seg, cnt8, off,
                         s).reshape(b, s, d)
    return x
```

```python
import functools

import jax
import jax.numpy as jnp
from jax import lax
from jax.experimental import pallas as pl
from jax.experimental.pallas import tpu as pltpu

EPS = 1e-6
F32 = jnp.float32
BF16 = jnp.bfloat16

SUBLANES = 8
LANES = 128
MXU_N = 256
_DONE = object()

POOL_WINDOWS = (2, 4, 8, 16)
POOL_GROUP = 64
CONV_KERNEL = 31
HEAD_DIM = 64
CHUNK = 128
N_EXPERTS = 8
HALO = 16

MIX_TM = 512
MIX_SUBTILES = 2
FFN_TM = 512
FFN_FC = 768
ROUTE_T = 512
MOE_TM = 768
MOE_FSTEP = 1792
MOE_FC = 512
CONV_STRIDE = 4
VMEM_LIMIT = 48 * 1024 * 1024


def _dot(a, b):
    return jnp.dot(a, b, preferred_element_type=F32)


def _dot_tn(a, b):
    return lax.dot_general(a, b, (((0,), (0,)), ((), ())), preferred_element_type=F32)


def _sigmoid(x):
    return 1.0 / (1.0 + jnp.exp(-x))


def _rms_mod(x, g_row, scale_row, shift_row):
    ms = jnp.mean(x * x, axis=-1, keepdims=True)
    return x * lax.rsqrt(ms + EPS) * (g_row * (1.0 + scale_row)) + shift_row


def _layer_norm(x, g_row, b_row):
    mu = jnp.mean(x, axis=-1, keepdims=True)
    xc = x - mu
    var = jnp.mean(xc * xc, axis=-1, keepdims=True)
    return xc * lax.rsqrt(var + EPS) * g_row + b_row


def _mod_kernel(c_ref, w_ref, b_ref, o_ref):
    c = c_ref[...]
    cond = (c * _sigmoid(c)).astype(BF16)
    o_ref[0] = _dot(cond, w_ref[0].astype(BF16)) + b_ref[0]


def _modulation(c, ada_w, ada_b):
    depth, d, nd = ada_w.shape
    b = c.shape[0]
    nblk = nd // d
    return pl.pallas_call(
        _mod_kernel,
        grid=(depth, nblk),
        in_specs=[
            pl.BlockSpec((b, d), lambda l, n: (0, 0)),
            pl.BlockSpec((1, d, d), lambda l, n: (l, 0, n)),
            pl.BlockSpec((1, 1, d), lambda l, n: (l, 0, n)),
        ],
        out_specs=pl.BlockSpec((1, b, d), lambda l, n: (l, 0, n)),
        out_shape=jax.ShapeDtypeStruct((depth, b, nd), F32),
        name="mod",
    )(c, ada_w, ada_b.reshape(depth, 1, nd))


def _route_tables(h, rw, rb, tri):
    t = h.shape[0]
    logits = _dot(h, rw)
    lt = logits.T[0:N_EXPERTS] + rb
    eid = lax.broadcasted_iota(jnp.int32, (N_EXPERTS, t), 0).astype(F32)
    m1 = jnp.max(lt, axis=0, keepdims=True)
    i1 = jnp.min(jnp.where(lt == m1, eid, float(N_EXPERTS)), axis=0, keepdims=True)
    sel1 = eid == i1
    lt2 = jnp.where(sel1, -jnp.inf, lt)
    m2 = jnp.max(lt2, axis=0, keepdims=True)
    i2 = jnp.min(jnp.where(lt2 == m2, eid, float(N_EXPERTS)), axis=0, keepdims=True)
    sel2 = eid == i2
    e21 = jnp.exp(m2 - m1)
    w1 = 1.0 / (1.0 + e21)
    w2 = e21 / (1.0 + e21)
    count = jnp.where(sel1 | sel2, 1.0, 0.0)
    rank = _dot(count.astype(BF16), tri)
    r1 = jnp.sum(jnp.where(sel1, rank, 0.0), axis=0, keepdims=True)
    r2 = jnp.sum(jnp.where(sel2, rank, 0.0), axis=0, keepdims=True)
    out = jnp.zeros((N_EXPERTS, t), F32)
    for k, rowv in enumerate((i1, i2, w1, w2, r1, r2)):
        out = jnp.where(eid == float(k), rowv, out)
    return out, jnp.sum(count, axis=1, keepdims=True)


def _mix_kernel(xm_ref, xp_ref, xn_ref, mod_ref, g1_ref, g2_ref, win_ref, pbd_ref, pscale_ref,
                cw_ref, cb_ref, clg_ref, clb_ref, slg_ref, slb_ref, swcat_ref, sbias_ref, wout_ref,
                *rest, tm, seq, n_sub, with_route):
    if with_route:
        rw_ref, rb_ref, tri_ref, xo_ref, ho_ref, route_ref, routec_ref, cnt_ref = rest[:8]
    else:
        xo_ref, ho_ref = rest[:2]
    glu_ref, ystage_ref, mixed_ref = rest[-3:]
    j = pl.program_id(1)
    nj = pl.num_programs(1)
    st = tm // n_sub
    ext = st + 2 * HALO
    pool_w = pbd_ref.shape[0]
    conv_w = cw_ref.shape[1]
    sgu_w = slg_ref.shape[1]
    pc_w = pool_w + 2 * conv_w
    n_slab = conv_w // LANES
    span = SUBLANES * CONV_STRIDE

    mod = mod_ref[0]
    shift_m, scale_m, gate_m = mod[0:1], mod[1:2], mod[2:3]
    shift_f, scale_f = mod[3:4], mod[4:5]
    cb = [cb_ref[:, m * LANES:(m + 1) * LANES] for m in range(n_slab)]
    clg = [clg_ref[:, m * LANES:(m + 1) * LANES] for m in range(n_slab)]
    clb = [clb_ref[:, m * LANES:(m + 1) * LANES] for m in range(n_slab)]
    lane = lax.broadcasted_iota(jnp.int32, (1, pool_w), 1)
    half = jnp.where(lane < POOL_GROUP, 1,
                     jnp.where(lane < 2 * POOL_GROUP, 2, jnp.where(lane < 3 * POOL_GROUP, 4, 8)))
    inv_full = 1.0 / (2 * half).astype(F32)

    def edge_inv(t0):
        t = lax.broadcasted_iota(jnp.int32, (SUBLANES, pool_w), 0) + t0
        return 1.0 / (jnp.minimum(t + half, seq) - jnp.maximum(t - half, 0)).astype(F32)

    state = {}

    def project(sub):
        r0 = sub * st
        first, last = sub == 0, sub == n_sub - 1
        xm = xm_ref[0, r0:r0 + st, :]
        xprev = xp_ref[0] if first else xm_ref[0, r0 - HALO:r0, :]
        xnext = xn_ref[0] if last else xm_ref[0, r0 + st:r0 + st + HALO, :]
        xe = jnp.concatenate([xprev, xm, xnext], axis=0)
        hn = _rms_mod(xe, g1_ref[...], scale_m, shift_m)
        if first or last:
            r = lax.broadcasted_iota(jnp.int32, (ext, 1), 0)
            ok = None
            if first:
                ok = (r >= HALO) | (j > 0)
            if last:
                ok_n = (r < HALO + st) | (j < nj - 1)
                ok = ok_n if ok is None else ok & ok_n
            hn = jnp.where(ok, hn, 0.0)
        hb = hn.astype(BF16)

        cols = []
        for c in range(0, pc_w, MXU_N):
            cols.append(_dot(hb, win_ref[:, c:c + MXU_N]))
            yield
        p1 = jnp.concatenate(cols, axis=1)
        hbm = hb[HALO:HALO + st]
        cols = []
        for c in range(pc_w, pc_w + 2 * sgu_w, MXU_N):
            cols.append(_dot(hbm, win_ref[:, c:c + MXU_N]))
            yield
        p2 = jnp.concatenate(cols, axis=1)
        val = p1[:, pool_w:pool_w + conv_w]
        gate = p1[:, pool_w + conv_w:pc_w]
        glu = val * _sigmoid(gate)
        for m in range(n_slab):
            glu_ref[sub, m] = glu[:, m * LANES:(m + 1) * LANES]
        state[sub] = (xm, p1[:, 0:pool_w], p2)

    def mixers(sub):
        r0 = sub * st
        first, last = sub == 0, sub == n_sub - 1
        xm, a, p2 = state[sub]

        s2 = a[0:ext - 1] + a[1:ext]
        s4 = s2[0:ext - 3] + s2[2:ext - 1]
        a_hi = s4[:, LANES:pool_w]
        s8 = a_hi[0:ext - 7] + a_hi[4:ext - 3]
        s16 = s8[0:ext - 15] + s8[8:ext - 7]
        is_first_group = lax.broadcasted_iota(jnp.int32, (st, LANES), 1) < POOL_GROUP
        win_lo = jnp.where(is_first_group, s2[HALO - 1:HALO - 1 + st, 0:LANES],
                           s4[HALO - 2:HALO - 2 + st, 0:LANES])
        win_hi = jnp.where(is_first_group, s8[HALO - 4:HALO - 4 + st], s16[HALO - 8:HALO - 8 + st])
        win = jnp.concatenate([win_lo, win_hi], axis=1)
        am = a[HALO:HALO + st]
        lo_rows = SUBLANES if first else 0
        hi_rows = SUBLANES if last else 0
        pieces = []
        if first:
            pieces.append(win[0:SUBLANES] * edge_inv(j * tm + r0) - am[0:SUBLANES])
        pieces.append(win[lo_rows:st - hi_rows] * inv_full - am[lo_rows:st - hi_rows])
        if last:
            pieces.append(win[st - SUBLANES:st] * edge_inv(j * tm + r0 + st - SUBLANES) - am[st - SUBLANES:st])
        diff = jnp.concatenate(pieces, axis=0) if len(pieces) > 1 else pieces[0]
        out_a = _dot(diff.astype(BF16), pbd_ref[...]) * pscale_ref[...]
        mixed_ref[r0:r0 + st, 0:pool_w] = out_a.astype(BF16)
        yield

        for blk in range(st // span):
            row0 = blk * span
            acc = [[None] * CONV_STRIDE for _ in range(n_slab)]
            for m in range(n_slab):
                w = {}
                for a in range(CONV_KERNEL + CONV_STRIDE - 1):
                    if a < CONV_KERNEL:
                        w[a] = cw_ref[a:a + 1, m * LANES:(m + 1) * LANES]
                    start = row0 + HALO - CONV_KERNEL // 2 + a
                    rows = glu_ref[sub, m, pl.ds(start, SUBLANES, stride=CONV_STRIDE), :]
                    for s in range(CONV_STRIDE):
                        k = a - s
                        if 0 <= k < CONV_KERNEL:
                            term = rows * w[k]
                            acc[m][s] = term if k == 0 else acc[m][s] + term
            for s in range(CONV_STRIDE):
                ys = [acc[m][s] + cb[m] for m in range(n_slab)]
                mu = sum(jnp.sum(y, axis=-1, keepdims=True) for y in ys) * (1.0 / conv_w)
                yc = [y - mu for y in ys]
                var = sum(jnp.sum(y * y, axis=-1, keepdims=True) for y in yc) * (1.0 / conv_w)
                rstd = lax.rsqrt(var + EPS)
                for m in range(n_slab):
                    yn = yc[m] * rstd * clg[m] + clb[m]
                    ystage_ref[sub, m, pl.ds(row0 + s, SUBLANES, stride=CONV_STRIDE), :] = yn * _sigmoid(yn)
            yield
        for m in range(n_slab):
            mixed_ref[r0:r0 + st, pool_w + m * LANES:pool_w + (m + 1) * LANES] = ystage_ref[sub, m].astype(BF16)

        u = p2[:, 0:sgu_w]
        v = _layer_norm(p2[:, sgu_w:2 * sgu_w], slg_ref[...], slb_ref[...])
        is_lo = (lax.broadcasted_iota(jnp.int32, (st, sgu_w), 1) % LANES) < HEAD_DIM
        v_lo = jnp.where(is_lo, v, 0.0).astype(BF16)
        v_hi = jnp.where(is_lo, 0.0, v).astype(BF16)
        sbias = sbias_ref[...]
        col0 = pool_w + conv_w
        for cp in range(st // (2 * CHUNK)):
            ra = 2 * cp * CHUNK
            rb = ra + CHUNK
            for m in range(sgu_w // LANES):
                ls = slice(m * LANES, (m + 1) * LANES)
                rhs = jnp.concatenate([
                    jnp.concatenate([v_lo[ra:ra + CHUNK, ls], v_lo[rb:rb + CHUNK, ls]], axis=1),
                    jnp.concatenate([v_hi[ra:ra + CHUNK, ls], v_hi[rb:rb + CHUNK, ls]], axis=1)], axis=0)
                mixed = _dot(swcat_ref[m], rhs)
                bias = sbias[:, ls]
                oa = u[ra:ra + CHUNK, ls] * (mixed[:, 0:LANES] + bias)
                ob = u[rb:rb + CHUNK, ls] * (mixed[:, LANES:2 * LANES] + bias)
                cs = slice(col0 + m * LANES, col0 + (m + 1) * LANES)
                mixed_ref[r0 + ra:r0 + ra + CHUNK, cs] = oa.astype(BF16)
                mixed_ref[r0 + rb:r0 + rb + CHUNK, cs] = ob.astype(BF16)
                yield

    def project_out(sub):
        r0 = sub * st
        xm = state[sub][0]
        mixed = mixed_ref[r0:r0 + st, :]
        cols = []
        for c in range(0, wout_ref.shape[1], MXU_N):
            res = _dot(mixed, wout_ref[:, c:c + MXU_N])
            cols.append(xm[:, c:c + MXU_N] + gate_m[:, c:c + MXU_N] * res)
            yield
        xnew = jnp.concatenate(cols, axis=1)
        xo_ref[0, r0:r0 + st, :] = xnew
        ho_ref[0, r0:r0 + st, :] = _rms_mod(xnew, g2_ref[...], scale_f, shift_f).astype(BF16)

    def emit(*stages):
        live = list(stages)
        while live:
            for gen in list(live):
                if next(gen, _DONE) is _DONE:
                    live.remove(gen)

    for step in range(n_sub + 2):
        emit(*([project_out(step - 2)] if 0 <= step - 2 < n_sub else []),
             *([mixers(step - 1)] if 0 <= step - 1 < n_sub else []),
             *([project(step)] if step < n_sub else []))

    if with_route:
        table, cnt = _route_tables(ho_ref[0], rw_ref[...], rb_ref[...], tri_ref[...])
        route_ref[0] = table
        routec_ref[0] = jnp.concatenate([table, jnp.zeros((LANES - N_EXPERTS, tm), F32)], axis=0).T
        cnt_ref[0] = jnp.broadcast_to(cnt, (N_EXPERTS, LANES))


def _mix_layer(x, mod, g1, g2, w_in, pool_bd, pool_scale, conv_w, conv_b, conv_ln_g, conv_ln_b,
               sgu_ln_g, sgu_ln_b, sgu_wcat, sgu_bias, w_out, router=None):
    b, s, d = x.shape
    tm = MIX_TM
    n_sub = MIX_SUBTILES
    st = tm // n_sub
    assert s % tm == 0 and tm % n_sub == 0 and st % (2 * CHUNK) == 0 and st % (SUBLANES * CONV_STRIDE) == 0
    n_slab = conv_w.shape[1] // LANES
    hb = tm // HALO
    nj = s // tm
    last_halo = s // HALO - 1
    full = lambda arr: pl.BlockSpec(arr.shape, lambda bi, j: (0,) * arr.ndim)
    params = [g1, g2, w_in, pool_bd, pool_scale, conv_w, conv_b, conv_ln_g, conv_ln_b,
              sgu_ln_g, sgu_ln_b, sgu_wcat, sgu_bias, w_out]
    out_specs = [
        pl.BlockSpec((1, tm, d), lambda bi, j: (bi, j, 0)),
        pl.BlockSpec((1, tm, d), lambda bi, j: (bi, j, 0)),
    ]
    out_shape = [jax.ShapeDtypeStruct((b, s, d), F32), jax.ShapeDtypeStruct((b, s, d), BF16)]
    if router is not None:
        assert tm == ROUTE_T
        params += list(router)
        tile = lambda bi, j: (bi * nj + j, 0, 0)
        out_specs += [pl.BlockSpec((1, N_EXPERTS, tm), tile), pl.BlockSpec((1, tm, LANES), tile),
                      pl.BlockSpec((1, N_EXPERTS, LANES), tile)]
        out_shape += [jax.ShapeDtypeStruct((b * nj, N_EXPERTS, tm), F32),
                      jax.ShapeDtypeStruct((b * nj, tm, LANES), F32),
                      jax.ShapeDtypeStruct((b * nj, N_EXPERTS, LANES), F32)]
    return pl.pallas_call(
        functools.partial(_mix_kernel, tm=tm, seq=s, n_sub=n_sub, with_route=router is not None),
        grid=(b, nj),
        in_specs=[
            pl.BlockSpec((1, tm, d), lambda bi, j: (bi, j, 0)),
            pl.BlockSpec((1, HALO, d), lambda bi, j: (bi, jnp.maximum(j * hb - 1, 0), 0)),
            pl.BlockSpec((1, HALO, d), lambda bi, j: (bi, jnp.minimum((j + 1) * hb, last_halo), 0)),
            pl.BlockSpec((1,) + mod.shape[1:], lambda bi, j: (bi, 0, 0)),
        ] + [full(p) for p in params],
        out_specs=out_specs,
        out_shape=out_shape,
        scratch_shapes=[
            pltpu.VMEM((n_sub, n_slab, st + 2 * HALO, LANES), F32),
            pltpu.VMEM((n_sub, n_slab, st, LANES), F32),
            pltpu.VMEM((tm, d), BF16),
        ],
        compiler_params=pltpu.CompilerParams(
            dimension_semantics=("arbitrary", "arbitrary"), vmem_limit_bytes=VMEM_LIMIT),
        name="mix",
    )(x, x, x, mod, *params)


def _ffn_kernel(x_ref, h_ref, mod_ref, wg_ref, wu_ref, wd_ref, o_ref, *, fc):
    h = h_ref[0]
    gate_f = mod_ref[0][5:6]
    width = wg_ref.shape[1]
    acc = None
    for lo in range(0, width, fc):
        fs = slice(lo, min(lo + fc, width))
        g = _dot(h, wg_ref[:, fs])
        u = _dot(h, wu_ref[:, fs])
        act = (g * _sigmoid(g) * u).astype(BF16)
        part = _dot(act, wd_ref[fs, :])
        acc = part if acc is None else acc + part
    o_ref[0] = x_ref[0] + gate_f * acc


def _dense_ffn(x, h, mod, wg, wu, wd):
    b, s, d = x.shape
    tm = FFN_TM
    assert s % tm == 0
    full = lambda arr: pl.BlockSpec(arr.shape, lambda bi, j: (0,) * arr.ndim)
    return pl.pallas_call(
        functools.partial(_ffn_kernel, fc=FFN_FC),
        grid=(b, s // tm),
        in_specs=[
            pl.BlockSpec((1, tm, d), lambda bi, j: (bi, j, 0)),
            pl.BlockSpec((1, tm, d), lambda bi, j: (bi, j, 0)),
            pl.BlockSpec((1,) + mod.shape[1:], lambda bi, j: (bi, 0, 0)),
            full(wg), full(wu), full(wd),
        ],
        out_specs=pl.BlockSpec((1, tm, d), lambda bi, j: (bi, j, 0)),
        out_shape=jax.ShapeDtypeStruct((b, s, d), F32),
        compiler_params=pltpu.CompilerParams(
            dimension_semantics=("arbitrary", "arbitrary"), vmem_limit_bytes=VMEM_LIMIT),
        name="ffn",
    )(x, h, mod, wg, wu, wd)


def _dest_rows(route, seg_ref, i):
    i1, i2, r1, r2 = route[0:1], route[1:2], route[4:5], route[5:6]
    s1 = jnp.zeros_like(i1)
    s2 = jnp.zeros_like(i2)
    for e in range(N_EXPERTS):
        start = seg_ref[i * N_EXPERTS + e].astype(F32)
        s1 = jnp.where(i1 == e, start, s1)
        s2 = jnp.where(i2 == e, start, s2)
    return (s1 + r1).astype(jnp.int32), (s2 + r2).astype(jnp.int32)


def _rows_copy(cnt, vrow, hrow, vmem_ref, hbm_ref, sem, max_rows, to_hbm, wait, enable=None):
    for k in range((max_rows // SUBLANES).bit_length()):
        size = SUBLANES << k
        done = (cnt >> (k + 4)) << (k + 4)
        take = ((cnt >> (k + 3)) & 1) == 1

        @pl.when(take if enable is None else take & enable)
        def _():
            v = vmem_ref.at[pl.ds(pl.multiple_of(vrow + done, SUBLANES), size)]
            g = hbm_ref.at[pl.ds(pl.multiple_of(hrow + done, SUBLANES), size)]
            cp = pltpu.make_async_copy(v, g, sem) if to_hbm else pltpu.make_async_copy(g, v, sem)
            if wait:
                cp.wait()
            else:
                cp.start()


def _segment_copies(i, seg_ref, cnt_ref, off_ref, vmem_ref, hbm_ref, sem, t, to_hbm, wait, enable=None):
    if wait:
        extra = sum(cnt_ref[i * N_EXPERTS + e] for e in range(N_EXPERTS)) - 2 * t
        v, g = vmem_ref.at[pl.ds(0, 2 * t)], hbm_ref.at[pl.ds(0, 2 * t)]
        whole = pltpu.make_async_copy(v, g, sem) if to_hbm else pltpu.make_async_copy(g, v, sem)
        if enable is None:
            whole.wait()
        else:
            pl.when(enable)(whole.wait)
        _rows_copy(extra, 0, 0, vmem_ref, hbm_ref, sem, N_EXPERTS * SUBLANES // 2, to_hbm, True, enable)
        return
    for e in range(N_EXPERTS):
        k = i * N_EXPERTS + e
        _rows_copy(cnt_ref[k], seg_ref[k], off_ref[k], vmem_ref, hbm_ref, sem, t, to_hbm, wait, enable)


def _dispatch_kernel(seg_ref, cnt_ref, off_ref, pad_ref, h_ref, route_ref, hs_ref, comp0_ref, comp1_ref,
                     sem, *, tm):
    i = pl.program_id(0)
    last = pl.num_programs(0) - 1
    t = h_ref.shape[0]
    rows = comp0_ref.shape[0]
    comps = (comp0_ref, comp1_ref)
    copies = functools.partial(_segment_copies, seg_ref=seg_ref, cnt_ref=cnt_ref, off_ref=off_ref,
                               hbm_ref=hs_ref, t=t, to_hbm=True)

    def step(slot):
        d1, d2 = _dest_rows(route_ref[0], seg_ref, i)
        rid = lax.broadcasted_iota(jnp.int32, (rows, t), 0)
        onehot = jnp.where((rid == d1) | (rid == d2), 1.0, 0.0).astype(BF16)
        gathered = _dot(onehot, h_ref[...])
        copies(jnp.maximum(i - 2, 0), vmem_ref=comps[slot], sem=sem.at[slot], wait=True, enable=i >= 2)
        comps[slot][...] = gathered
        copies(i, vmem_ref=comps[slot], sem=sem.at[slot], wait=False)

    for slot in range(2):
        pl.when(i % 2 == slot)(functools.partial(step, slot))

    @pl.when(i == last)
    def _():
        for slot in range(2):
            tile = jnp.where(last % 2 == slot, last, jnp.maximum(last - 1, 0))
            copies(tile, vmem_ref=comps[slot], sem=sem.at[slot], wait=True,
                   enable=(last % 2 == slot) | (last >= 1))
        comp0_ref[0:tm, :] = jnp.zeros((tm, comp0_ref.shape[1]), F32)
        zeros = comp0_ref
        for wait in (False, True):
            for e in range(N_EXPERTS):
                _rows_copy(pad_ref[N_EXPERTS + e], 0, pad_ref[e], zeros, hs_ref, sem.at[2], tm, True, wait)

        def tail(g, carry):
            row = pl.multiple_of(g * tm, SUBLANES)
            cp = pltpu.make_async_copy(zeros.at[pl.ds(0, tm)], hs_ref.at[pl.ds(row, tm)], sem.at[2])
            cp.start()
            cp.wait()
            return carry

        lax.fori_loop(pad_ref[2 * N_EXPERTS], pad_ref[2 * N_EXPERTS + 1], tail, 0)


def _dispatch(h2d, route, seg, cnt8, off, pad, total_rows):
    n, d = h2d.shape
    t = ROUTE_T
    nt = n // t
    rows = 2 * t + N_EXPERTS * SUBLANES
    assert rows >= MOE_TM
    return pl.pallas_call(
        functools.partial(_dispatch_kernel, tm=MOE_TM),
        grid_spec=pltpu.PrefetchScalarGridSpec(
            num_scalar_prefetch=4,
            grid=(nt,),
            in_specs=[
                pl.BlockSpec((t, d), lambda i, *_: (i, 0)),
                pl.BlockSpec((1, N_EXPERTS, t), lambda i, *_: (i, 0, 0)),
            ],
            out_specs=pl.BlockSpec(memory_space=pl.ANY),
            scratch_shapes=[pltpu.VMEM((rows, d), F32), pltpu.VMEM((rows, d), F32),
                            pltpu.SemaphoreType.DMA((3,))],
        ),
        out_shape=jax.ShapeDtypeStruct((total_rows, d), F32),
        compiler_params=pltpu.CompilerParams(
            dimension_semantics=("arbitrary",), vmem_limit_bytes=VMEM_LIMIT),
        name="dispatch",
    )(seg, cnt8, off, pad, h2d, route)


def _moe_kernel(be_ref, nv_ref, hs_ref, wg_ref, wu_ref, wd_ref, o_ref, hb_ref, *, fc):
    g = pl.program_id(0)
    f = pl.program_id(1)
    nvalid = nv_ref[g]

    @pl.when(nvalid > 0)
    def _():
        @pl.when(f == 0)
        def _():
            hb_ref[...] = hs_ref[...].astype(BF16)

        h = hb_ref[...]
        width = wg_ref.shape[2]
        acc = None
        for lo in range(0, width, fc):
            fs = slice(lo, min(lo + fc, width))
            gt = _dot(h, wg_ref[0, :, fs])
            up = _dot(h, wu_ref[0, :, fs])
            act = (gt * _sigmoid(gt) * up).astype(BF16)
            part = _dot(act, wd_ref[0, fs, :])
            acc = part if acc is None else acc + part

        @pl.when(f == 0)
        def _():
            o_ref[...] = acc

        @pl.when(f > 0)
        def _():
            o_ref[...] += acc

    @pl.when((nvalid == 0) & (f == 0))
    def _():
        o_ref[...] = jnp.zeros_like(o_ref)


def _moe(hs, blk_e, blk_nv, wg, wu, wd):
    rows = hs.shape[0]
    d = wg.shape[1]
    tm = MOE_TM
    fstep = MOE_FSTEP
    nf = wg.shape[2] // fstep
    nblk = rows // tm
    assert wg.shape[2] % fstep == 0

    def fidx(g, f, nv):
        return jnp.where(nv[g] > 0, f, nf - 1)

    return pl.pallas_call(
        functools.partial(_moe_kernel, fc=MOE_FC),
        grid_spec=pltpu.PrefetchScalarGridSpec(
            num_scalar_prefetch=2,
            grid=(nblk, nf),
            in_specs=[
                pl.BlockSpec((tm, d), lambda g, f, be, nv: (g, 0)),
                pl.BlockSpec((1, d, fstep), lambda g, f, be, nv: (be[g], 0, fidx(g, f, nv))),
                pl.BlockSpec((1, d, fstep), lambda g, f, be, nv: (be[g], 0, fidx(g, f, nv))),
                pl.BlockSpec((1, fstep, d), lambda g, f, be, nv: (be[g], fidx(g, f, nv), 0)),
            ],
            out_specs=pl.BlockSpec((tm, d), lambda g, f, be, nv: (g, 0)),
            scratch_shapes=[pltpu.VMEM((tm, d), BF16)],
        ),
        out_shape=jax.ShapeDtypeStruct((rows, d), F32),
        compiler_params=pltpu.CompilerParams(
            dimension_semantics=("arbitrary", "arbitrary"), vmem_limit_bytes=VMEM_LIMIT),
        name="moe",
    )(blk_e, blk_nv, hs, wg, wu, wd)


def _combine_kernel(seg_ref, cnt_ref, off_ref, x_ref, routec_ref, mod_ref, gfin_ref, ys_ref, o_ref,
                    comp0_ref, comp1_ref, sem):
    i = pl.program_id(0)
    last = pl.num_programs(0) - 1
    t = x_ref.shape[0]
    rows = comp0_ref.shape[0]
    comps = (comp0_ref, comp1_ref)
    fetch = functools.partial(_segment_copies, seg_ref=seg_ref, cnt_ref=cnt_ref, off_ref=off_ref,
                              hbm_ref=ys_ref, t=t, to_hbm=False)

    @pl.when(i == 0)
    def _():
        comp0_ref[...] = jnp.zeros_like(comp0_ref)
        comp1_ref[...] = jnp.zeros_like(comp1_ref)
        fetch(i, vmem_ref=comp0_ref, sem=sem.at[0], wait=False)

    def step(slot):
        fetch(i, vmem_ref=comps[slot], sem=sem.at[slot], wait=True)
        fetch(jnp.minimum(i + 1, last), vmem_ref=comps[1 - slot], sem=sem.at[1 - slot], wait=False,
              enable=i < last)
        rc = routec_ref[0]
        i1, i2, w1, w2, r1, r2 = (rc[:, k:k + 1] for k in range(6))
        s1 = jnp.zeros_like(i1)
        s2 = jnp.zeros_like(i2)
        for e in range(N_EXPERTS):
            start = seg_ref[i * N_EXPERTS + e].astype(F32)
            s1 = jnp.where(i1 == e, start, s1)
            s2 = jnp.where(i2 == e, start, s2)
        d1 = (s1 + r1).astype(jnp.int32)
        d2 = (s2 + r2).astype(jnp.int32)
        cid = lax.broadcasted_iota(jnp.int32, (t, rows), 1)
        wmat = (jnp.where(cid == d1, w1, 0.0) + jnp.where(cid == d2, w2, 0.0)).astype(BF16)
        y = _dot(wmat, comps[slot][...].astype(BF16))
        gate_f = mod_ref[0][5:6]
        x2 = x_ref[...] + gate_f * y
        ms = jnp.mean(x2 * x2, axis=-1, keepdims=True)
        o_ref[...] = x2 * lax.rsqrt(ms + EPS) * gfin_ref[...]

    for slot in range(2):
        pl.when(i % 2 == slot)(functools.partial(step, slot))


def _combine(x2d, routec, mod, gfin, ys, seg, cnt8, off, seq):
    n, d = x2d.shape
    t = ROUTE_T
    nt = n // t
    tiles_per_seq = seq // t
    rows = 2 * t + LANES
    return pl.pallas_call(
        _combine_kernel,
        grid_spec=pltpu.PrefetchScalarGridSpec(
            num_scalar_prefetch=3,
            grid=(nt,),
            in_specs=[
                pl.BlockSpec((t, d), lambda i, *_: (i, 0)),
                pl.BlockSpec((1, t, LANES), lambda i, *_: (i, 0, 0)),
                pl.BlockSpec((1,) + mod.shape[1:], lambda i, *_: (i // tiles_per_seq, 0, 0)),
                pl.BlockSpec(gfin.shape, lambda i, *_: (0, 0)),
                pl.BlockSpec(memory_space=pl.ANY),
            ],
            out_specs=pl.BlockSpec((t, d), lambda i, *_: (i, 0)),
            scratch_shapes=[pltpu.VMEM((rows, d), F32), pltpu.VMEM((rows, d), F32),
                            pltpu.SemaphoreType.DMA((2,))],
        ),
        out_shape=jax.ShapeDtypeStruct((n, d), F32),
        compiler_params=pltpu.CompilerParams(
            dimension_semantics=("arbitrary",), vmem_limit_bytes=VMEM_LIMIT),
        name="combine",
    )(seg, cnt8, off, x2d, routec, mod, gfin, ys)


def _moe_plan(cnt, tm):
    nt = cnt.shape[0]
    cnt8 = (cnt + SUBLANES - 1) // SUBLANES * SUBLANES
    seg = jnp.cumsum(cnt8, axis=1) - cnt8
    tot = jnp.sum(cnt8, axis=0)
    nblk = (tot + tm - 1) // tm
    blk_end = jnp.cumsum(nblk)
    blk_start = blk_end - nblk
    off = (blk_start * tm)[None, :] + jnp.cumsum(cnt8, axis=0) - cnt8
    max_rows = 2 * nt * ROUTE_T + nt * N_EXPERTS * (SUBLANES - 1)
    n_blocks = (max_rows + tm - 1) // tm + N_EXPERTS
    g = jnp.arange(n_blocks, dtype=jnp.int32)
    blk_e = jnp.minimum(jnp.sum(g[:, None] >= blk_end[None, :], axis=1), N_EXPERTS - 1).astype(jnp.int32)
    blk_nv = jnp.clip(tot[blk_e] - (g - blk_start[blk_e]) * tm, 0, tm).astype(jnp.int32)
    pad = jnp.concatenate([blk_start * tm + tot, nblk * tm - tot, blk_end[-1:],
                           jnp.full((1,), n_blocks, jnp.int32)])
    flat = lambda a: a.reshape(-1).astype(jnp.int32)
    return flat(seg), flat(cnt8), flat(off), flat(pad), blk_e, blk_nv, n_blocks * tm


def kernel(x, c, ada_w, ada_b, mix_norm_g, ffn_norm_g, w_in, pool_w, pool_scale, conv_w, conv_b,
           conv_ln_g, conv_ln_b, sgu_ln_g, sgu_ln_b, sgu_w, sgu_b, w_out, ffn_w_gate, ffn_w_up,
           ffn_w_down, router_w, router_b, moe_w_gate, moe_w_up, moe_w_down, final_norm_g):
    b, s, d = x.shape
    depth = ada_w.shape[0]
    n_mod = ada_w.shape[2] // d
    mod_all = _modulation(c, ada_w, ada_b).reshape(depth, b, n_mod, d)
    row = lambda v: v.reshape(1, -1)

    for l in range(depth):
        mod = mod_all[l]
        pool_bd = jax.scipy.linalg.block_diag(*[pool_w[l, g] for g in range(pool_w.shape[1])]).astype(BF16)
        sw = sgu_w[l]
        sgu_wcat = jnp.concatenate([sw[0::2], sw[1::2]], axis=2).astype(BF16)
        sgu_bias = jnp.repeat(sgu_b[l].T, HEAD_DIM, axis=1)
        i = l // 2
        router = None
        if l % 2 == 1:
            rw_pad = jnp.pad(router_w[i], ((0, 0), (0, LANES - N_EXPERTS))).astype(BF16)
            tri = jnp.triu(jnp.ones((ROUTE_T, ROUTE_T), BF16), k=1)
            router = (rw_pad, router_b[i].reshape(N_EXPERTS, 1), tri)
        x, h, *tables = _mix_layer(x, mod, row(mix_norm_g[l]), row(ffn_norm_g[l]), w_in[l].astype(BF16),
                                   pool_bd, row(pool_scale[l]), conv_w[l], row(conv_b[l]),
                                   row(conv_ln_g[l]), row(conv_ln_b[l]), row(sgu_ln_g[l]),
                                   row(sgu_ln_b[l]), sgu_wcat, sgu_bias, w_out[l].astype(BF16), router)
        if l % 2 == 0:
            x = _dense_ffn(x, h, mod, ffn_w_gate[i].astype(BF16), ffn_w_up[i].astype(BF16),
                           ffn_w_down[i].astype(BF16))
        else:
            assert l == depth - 1, "the combine kernel applies the final norm"
            h2d = h.reshape(b * s, d)
            route, routec, cnt = tables
            seg, cnt8, off, pad, blk_e, blk_nv, total_rows = _moe_plan(
                cnt[:, :, 0].astype(jnp.int32), MOE_TM)
            hs = _dispatch(h2d, route, seg, cnt8, off, pad, total_rows)
            ys = _moe(hs, blk_e, blk_nv, moe_w_gate[i].astype(BF16), moe_w_up[i].astype(BF16),
                      moe_w_down[i].astype(BF16))
            x = _combine(x.reshape(b * s, d), routec, mod, row(final_norm_g), ys, seg, cnt8, off,
                         s).reshape(b, s, d)
    return x
```

```python
import functools

import jax
import jax.numpy as jnp
from jax import lax
from jax.experimental import pallas as pl
from jax.experimental.pallas import tpu as pltpu

EPS = 1e-6
F32 = jnp.float32
BF16 = jnp.bfloat16

SUBLANES = 8
LANES = 128
MXU_N = 256
_DONE = object()

POOL_WINDOWS = (2, 4, 8, 16)
POOL_GROUP = 64
CONV_KERNEL = 31
HEAD_DIM = 64
CHUNK = 128
N_EXPERTS = 8
HALO = 16

MIX_TM = 512
MIX_SUBTILES = 2
FFN_TM = 512
FFN_FC = 768
ROUTE_T = 512
MOE_TM = 768
MOE_FSTEP = 1792
MOE_FC = 512
CONV_STRIDE = 4
VMEM_LIMIT = 48 * 1024 * 1024


def _dot(a, b):
    return jnp.dot(a, b, preferred_element_type=F32)


def _dot_tn(a, b):
    return lax.dot_general(a, b, (((0,), (0,)), ((), ())), preferred_element_type=F32)


def _sigmoid(x):
    return 1.0 / (1.0 + jnp.exp(-x))


def _rms_mod(x, g_row, scale_row, shift_row):
    ms = jnp.mean(x * x, axis=-1, keepdims=True)
    return x * lax.rsqrt(ms + EPS) * (g_row * (1.0 + scale_row)) + shift_row


def _layer_norm(x, g_row, b_row):
    mu = jnp.mean(x, axis=-1, keepdims=True)
    xc = x - mu
    var = jnp.mean(xc * xc, axis=-1, keepdims=True)
    return xc * lax.rsqrt(var + EPS) * g_row + b_row


def _mod_kernel(c_ref, w_ref, b_ref, o_ref):
    c = c_ref[...]
    cond = (c * _sigmoid(c)).astype(BF16)
    o_ref[0] = _dot(cond, w_ref[0].astype(BF16)) + b_ref[0]


def _modulation(c, ada_w, ada_b):
    depth, d, nd = ada_w.shape
    b = c.shape[0]
    nblk = nd // d
    return pl.pallas_call(
        _mod_kernel,
        grid=(depth, nblk),
        in_specs=[
            pl.BlockSpec((b, d), lambda l, n: (0, 0)),
            pl.BlockSpec((1, d, d), lambda l, n: (l, 0, n)),
            pl.BlockSpec((1, 1, d), lambda l, n: (l, 0, n)),
        ],
        out_specs=pl.BlockSpec((1, b, d), lambda l, n: (l, 0, n)),
        out_shape=jax.ShapeDtypeStruct((depth, b, nd), F32),
        name="mod",
    )(c, ada_w, ada_b.reshape(depth, 1, nd))


def _route_tables(h, rw, rb, tri):
    t = h.shape[0]
    logits = _dot(h, rw)
    lt = logits.T[0:N_EXPERTS] + rb
    eid = lax.broadcasted_iota(jnp.int32, (N_EXPERTS, t), 0).astype(F32)
    m1 = jnp.max(lt, axis=0, keepdims=True)
    i1 = jnp.min(jnp.where(lt == m1, eid, float(N_EXPERTS)), axis=0, keepdims=True)
    sel1 = eid == i1
    lt2 = jnp.where(sel1, -jnp.inf, lt)
    m2 = jnp.max(lt2, axis=0, keepdims=True)
    i2 = jnp.min(jnp.where(lt2 == m2, eid, float(N_EXPERTS)), axis=0, keepdims=True)
    sel2 = eid == i2
    e21 = jnp.exp(m2 - m1)
    w1 = 1.0 / (1.0 + e21)
    w2 = e21 / (1.0 + e21)
    count = jnp.where(sel1 | sel2, 1.0, 0.0)
    rank = _dot(count.astype(BF16), tri)
    r1 = jnp.sum(jnp.where(sel1, rank, 0.0), axis=0, keepdims=True)
    r2 = jnp.sum(jnp.where(sel2, rank, 0.0), axis=0, keepdims=True)
    out = jnp.zeros((N_EXPERTS, t), F32)
    for k, rowv in enumerate((i1, i2, w1, w2, r1, r2)):
        out = jnp.where(eid == float(k), rowv, out)
    return out, jnp.sum(count, axis=1, keepdims=True)


def _mix_kernel(xm_ref, xp_ref, xn_ref, mod_ref, g1_ref, g2_ref, win_ref, pbd_ref, pscale_ref,
                cw_ref, cb_ref, clg_ref, clb_ref, slg_ref, slb_ref, swcat_ref, sbias_ref, wout_ref,
                *rest, tm, seq, n_sub, with_route):
    if with_route:
        rw_ref, rb_ref, tri_ref, xo_ref, ho_ref, route_ref, routec_ref, cnt_ref = rest[:8]
    else:
        xo_ref, ho_ref = rest[:2]
    glu_ref, ystage_ref, mixed_ref = rest[-3:]
    j = pl.program_id(1)
    nj = pl.num_programs(1)
    st = tm // n_sub
    ext = st + 2 * HALO
    pool_w = pbd_ref.shape[0]
    conv_w = cw_ref.shape[1]
    sgu_w = slg_ref.shape[1]
    pc_w = pool_w + 2 * conv_w
    n_slab = conv_w // LANES
    span = SUBLANES * CONV_STRIDE

    mod = mod_ref[0]
    shift_m, scale_m, gate_m = mod[0:1], mod[1:2], mod[2:3]
    shift_f, scale_f = mod[3:4], mod[4:5]
    cb = [cb_ref[:, m * LANES:(m + 1) * LANES] for m in range(n_slab)]
    clg = [clg_ref[:, m * LANES:(m + 1) * LANES] for m in range(n_slab)]
    clb = [clb_ref[:, m * LANES:(m + 1) * LANES] for m in range(n_slab)]
    lane = lax.broadcasted_iota(jnp.int32, (1, pool_w), 1)
    half = jnp.where(lane < POOL_GROUP, 1,
                     jnp.where(lane < 2 * POOL_GROUP, 2, jnp.where(lane < 3 * POOL_GROUP, 4, 8)))
    inv_full = 1.0 / (2 * half).astype(F32)

    def edge_inv(t0):
        t = lax.broadcasted_iota(jnp.int32, (SUBLANES, pool_w), 0) + t0
        return 1.0 / (jnp.minimum(t + half, seq) - jnp.maximum(t - half, 0)).astype(F32)

    state = {}

    def project(sub):
        r0 = sub * st
        first, last = sub == 0, sub == n_sub - 1
        xm = xm_ref[0, r0:r0 + st, :]
        xprev = xp_ref[0] if first else xm_ref[0, r0 - HALO:r0, :]
        xnext = xn_ref[0] if last else xm_ref[0, r0 + st:r0 + st + HALO, :]
        xe = jnp.concatenate([xprev, xm, xnext], axis=0)
        hn = _rms_mod(xe, g1_ref[...], scale_m, shift_m)
        if first or last:
            r = lax.broadcasted_iota(jnp.int32, (ext, 1), 0)
            ok = None
            if first:
                ok = (r >= HALO) | (j > 0)
            if last:
                ok_n = (r < HALO + st) | (j < nj - 1)
                ok = ok_n if ok is None else ok & ok_n
            hn = jnp.where(ok, hn, 0.0)
        hb = hn.astype(BF16)

        cols = []
        for c in range(0, pc_w, MXU_N):
            cols.append(_dot(hb, win_ref[:, c:c + MXU_N]))
            yield
        p1 = jnp.concatenate(cols, axis=1)
        hbm = hb[HALO:HALO + st]
        cols = []
        for c in range(pc_w, pc_w + 2 * sgu_w, MXU_N):
            cols.append(_dot(hbm, win_ref[:, c:c + MXU_N]))
            yield
        p2 = jnp.concatenate(cols, axis=1)
        val = p1[:, pool_w:pool_w + conv_w]
        gate = p1[:, pool_w + conv_w:pc_w]
        glu = val * _sigmoid(gate)
        for m in range(n_slab):
            glu_ref[sub, m] = glu[:, m * LANES:(m + 1) * LANES]
        state[sub] = (xm, p1[:, 0:pool_w], p2)

    def mixers(sub):
        r0 = sub * st
        first, last = sub == 0, sub == n_sub - 1
        xm, a, p2 = state[sub]

        s2 = a[0:ext - 1] + a[1:ext]
        s4 = s2[0:ext - 3] + s2[2:ext - 1]
        a_hi = s4[:, LANES:pool_w]
        s8 = a_hi[0:ext - 7] + a_hi[4:ext - 3]
        s16 = s8[0:ext - 15] + s8[8:ext - 7]
        is_first_group = lax.broadcasted_iota(jnp.int32, (st, LANES), 1) < POOL_GROUP
        win_lo = jnp.where(is_first_group, s2[HALO - 1:HALO - 1 + st, 0:LANES],
                           s4[HALO - 2:HALO - 2 + st, 0:LANES])
        win_hi = jnp.where(is_first_group, s8[HALO - 4:HALO - 4 + st], s16[HALO - 8:HALO - 8 + st])
        win = jnp.concatenate([win_lo, win_hi], axis=1)
        am = a[HALO:HALO + st]
        lo_rows = SUBLANES if first else 0
        hi_rows = SUBLANES if last else 0
        pieces = []
        if first:
            pieces.append(win[0:SUBLANES] * edge_inv(j * tm + r0) - am[0:SUBLANES])
        pieces.append(win[lo_rows:st - hi_rows] * inv_full - am[lo_rows:st - hi_rows])
        if last:
            pieces.append(win[st - SUBLANES:st] * edge_inv(j * tm + r0 + st - SUBLANES) - am[st - SUBLANES:st])
        diff = jnp.concatenate(pieces, axis=0) if len(pieces) > 1 else pieces[0]
        out_a = _dot(diff.astype(BF16), pbd_ref[...]) * pscale_ref[...]
        mixed_ref[r0:r0 + st, 0:pool_w] = out_a.astype(BF16)
        yield

        for blk in range(st // span):
            row0 = blk * span
            acc = [[None] * CONV_STRIDE for _ in range(n_slab)]
            for m in range(n_slab):
                w = {}
                for a in range(CONV_KERNEL + CONV_STRIDE - 1):
                    if a < CONV_KERNEL:
                        w[a] = cw_ref[a:a + 1, m * LANES:(m + 1) * LANES]
                    start = row0 + HALO - CONV_KERNEL // 2 + a
                    rows = glu_ref[sub, m, pl.ds(start, SUBLANES, stride=CONV_STRIDE), :]
                    for s in range(CONV_STRIDE):
                        k = a - s
                        if 0 <= k < CONV_KERNEL:
                            term = rows * w[k]
                            acc[m][s] = term if k == 0 else acc[m][s] + term
            for s in range(CONV_STRIDE):
                ys = [acc[m][s] + cb[m] for m in range(n_slab)]
                mu = sum(jnp.sum(y, axis=-1, keepdims=True) for y in ys) * (1.0 / conv_w)
                yc = [y - mu for y in ys]
                var = sum(jnp.sum(y * y, axis=-1, keepdims=True) for y in yc) * (1.0 / conv_w)
                rstd = lax.rsqrt(var + EPS)
                for m in range(n_slab):
                    yn = yc[m] * rstd * clg[m] + clb[m]
                    ystage_ref[sub, m, pl.ds(row0 + s, SUBLANES, stride=CONV_STRIDE), :] = yn * _sigmoid(yn)
            yield
        for m in range(n_slab):
            mixed_ref[r0:r0 + st, pool_w + m * LANES:pool_w + (m + 1) * LANES] = ystage_ref[sub, m].astype(BF16)

        u = p2[:, 0:sgu_w]
        v = _layer_norm(p2[:, sgu_w:2 * sgu_w], slg_ref[...], slb_ref[...])
        is_lo = (lax.broadcasted_iota(jnp.int32, (st, sgu_w), 1) % LANES) < HEAD_DIM
        v_lo = jnp.where(is_lo, v, 0.0).astype(BF16)
        v_hi = jnp.where(is_lo, 0.0, v).astype(BF16)
        sbias = sbias_ref[...]
        col0 = pool_w + conv_w
        for cp in range(st // (2 * CHUNK)):
            ra = 2 * cp * CHUNK
            rb = ra + CHUNK
            for m in range(sgu_w // LANES):
                ls = slice(m * LANES, (m + 1) * LANES)
                rhs = jnp.concatenate([
                    jnp.concatenate([v_lo[ra:ra + CHUNK, ls], v_lo[rb:rb + CHUNK, ls]], axis=1),
                    jnp.concatenate([v_hi[ra:ra + CHUNK, ls], v_hi[rb:rb + CHUNK, ls]], axis=1)], axis=0)
                mixed = _dot(swcat_ref[m], rhs)
                bias = sbias[:, ls]
                oa = u[ra:ra + CHUNK, ls] * (mixed[:, 0:LANES] + bias)
                ob = u[rb:rb + CHUNK, ls] * (mixed[:, LANES:2 * LANES] + bias)
                cs = slice(col0 + m * LANES, col0 + (m + 1) * LANES)
                mixed_ref[r0 + ra:r0 + ra + CHUNK, cs] = oa.astype(BF16)
                mixed_ref[r0 + rb:r0 + rb + CHUNK, cs] = ob.astype(BF16)
                yield

    def project_out(sub):
        r0 = sub * st
        xm = state[sub][0]
        mixed = mixed_ref[r0:r0 + st, :]
        cols = []
        for c in range(0, wout_ref.shape[1], MXU_N):
            res = _dot(mixed, wout_ref[:, c:c + MXU_N])
            cols.append(xm[:, c:c + MXU_N] + gate_m[:, c:c + MXU_N] * res)
            yield
        xnew = jnp.concatenate(cols, axis=1)
        xo_ref[0, r0:r0 + st, :] = xnew
        ho_ref[0, r0:r0 + st, :] = _rms_mod(xnew, g2_ref[...], scale_f, shift_f).astype(BF16)

    def emit(*stages):
        live = list(stages)
        while live:
            for gen in list(live):
                if next(gen, _DONE) is _DONE:
                    live.remove(gen)

    for step in range(n_sub + 2):
        emit(*([project_out(step - 2)] if 0 <= step - 2 < n_sub else []),
             *([mixers(step - 1)] if 0 <= step - 1 < n_sub else []),
             *([project(step)] if step < n_sub else []))

    if with_route:
        table, cnt = _route_tables(ho_ref[0], rw_ref[...], rb_ref[...], tri_ref[...])
        route_ref[0] = table
        routec_ref[0] = jnp.concatenate([table, jnp.zeros((LANES - N_EXPERTS, tm), F32)], axis=0).T
        cnt_ref[0] = jnp.broadcast_to(cnt, (N_EXPERTS, LANES))


def _mix_layer(x, mod, g1, g2, w_in, pool_bd, pool_scale, conv_w, conv_b, conv_ln_g, conv_ln_b,
               sgu_ln_g, sgu_ln_b, sgu_wcat, sgu_bias, w_out, router=None):
    b, s, d = x.shape
    tm = MIX_TM
    n_sub = MIX_SUBTILES
    st = tm // n_sub
    assert s % tm == 0 and tm % n_sub == 0 and st % (2 * CHUNK) == 0 and st % (SUBLANES * CONV_STRIDE) == 0
    n_slab = conv_w.shape[1] // LANES
    hb = tm // HALO
    nj = s // tm
    last_halo = s // HALO - 1
    full = lambda arr: pl.BlockSpec(arr.shape, lambda bi, j: (0,) * arr.ndim)
    params = [g1, g2, w_in, pool_bd, pool_scale, conv_w, conv_b, conv_ln_g, conv_ln_b,
              sgu_ln_g, sgu_ln_b, sgu_wcat, sgu_bias, w_out]
    out_specs = [
        pl.BlockSpec((1, tm, d), lambda bi, j: (bi, j, 0)),
        pl.BlockSpec((1, tm, d), lambda bi, j: (bi, j, 0)),
    ]
    out_shape = [jax.ShapeDtypeStruct((b, s, d), F32), jax.ShapeDtypeStruct((b, s, d), BF16)]
    if router is not None:
        assert tm == ROUTE_T
        params += list(router)
        tile = lambda bi, j: (bi * nj + j, 0, 0)
        out_specs += [pl.BlockSpec((1, N_EXPERTS, tm), tile), pl.BlockSpec((1, tm, LANES), tile),
                      pl.BlockSpec((1, N_EXPERTS, LANES), tile)]
        out_shape += [jax.ShapeDtypeStruct((b * nj, N_EXPERTS, tm), F32),
                      jax.ShapeDtypeStruct((b * nj, tm, LANES), F32),
                      jax.ShapeDtypeStruct((b * nj, N_EXPERTS, LANES), F32)]
    return pl.pallas_call(
        functools.partial(_mix_kernel, tm=tm, seq=s, n_sub=n_sub, with_route=router is not None),
        grid=(b, nj),
        in_specs=[
            pl.BlockSpec((1, tm, d), lambda bi, j: (bi, j, 0)),
            pl.BlockSpec((1, HALO, d), lambda bi, j: (bi, jnp.maximum(j * hb - 1, 0), 0)),
            pl.BlockSpec((1, HALO, d), lambda bi, j: (bi, jnp.minimum((j + 1) * hb, last_halo), 0)),
            pl.BlockSpec((1,) + mod.shape[1:], lambda bi, j: (bi, 0, 0)),
        ] + [full(p) for p in params],
        out_specs=out_specs,
        out_shape=out_shape,
        scratch_shapes=[
            pltpu.VMEM((n_sub, n_slab, st + 2 * HALO, LANES), F32),
            pltpu.VMEM((n_sub, n_slab, st, LANES), F32),
            pltpu.VMEM((tm, d), BF16),
        ],
        compiler_params=pltpu.CompilerParams(
            dimension_semantics=("arbitrary", "arbitrary"), vmem_limit_bytes=VMEM_LIMIT),
        name="mix",
    )(x, x, x, mod, *params)


def _ffn_kernel(x_ref, h_ref, mod_ref, wg_ref, wu_ref, wd_ref, o_ref, *, fc):
    h = h_ref[0]
    gate_f = mod_ref[0][5:6]
    width = wg_ref.shape[1]
    acc = None
    for lo in range(0, width, fc):
        fs = slice(lo, min(lo + fc, width))
        g = _dot(h, wg_ref[:, fs])
        u = _dot(h, wu_ref[:, fs])
        act = (g * _sigmoid(g) * u).astype(BF16)
        part = _dot(act, wd_ref[fs, :])
        acc = part if acc is None else acc + part
    o_ref[0] = x_ref[0] + gate_f * acc


def _dense_ffn(x, h, mod, wg, wu, wd):
    b, s, d = x.shape
    tm = FFN_TM
    assert s % tm == 0
    full = lambda arr: pl.BlockSpec(arr.shape, lambda bi, j: (0,) * arr.ndim)
    return pl.pallas_call(
        functools.partial(_ffn_kernel, fc=FFN_FC),
        grid=(b, s // tm),
        in_specs=[
            pl.BlockSpec((1, tm, d), lambda bi, j: (bi, j, 0)),
            pl.BlockSpec((1, tm, d), lambda bi, j: (bi, j, 0)),
            pl.BlockSpec((1,) + mod.shape[1:], lambda bi, j: (bi, 0, 0)),
            full(wg), full(wu), full(wd),
        ],
        out_specs=pl.BlockSpec((1, tm, d), lambda bi, j: (bi, j, 0)),
        out_shape=jax.ShapeDtypeStruct((b, s, d), F32),
        compiler_params=pltpu.CompilerParams(
            dimension_semantics=("arbitrary", "arbitrary"), vmem_limit_bytes=VMEM_LIMIT),
        name="ffn",
    )(x, h, mod, wg, wu, wd)


def _dest_rows(route, seg_ref, i):
    i1, i2, r1, r2 = route[0:1], route[1:2], route[4:5], route[5:6]
    s1 = jnp.zeros_like(i1)
    s2 = jnp.zeros_like(i2)
    for e in range(N_EXPERTS):
        start = seg_ref[i * N_EXPERTS + e].astype(F32)
        s1 = jnp.where(i1 == e, start, s1)
        s2 = jnp.where(i2 == e, start, s2)
    return (s1 + r1).astype(jnp.int32), (s2 + r2).astype(jnp.int32)


def _rows_copy(cnt, vrow, hrow, vmem_ref, hbm_ref, sem, max_rows, to_hbm, wait, enable=None):
    for k in range((max_rows // SUBLANES).bit_length()):
        size = SUBLANES << k
        done = (cnt >> (k + 4)) << (k + 4)
        take = ((cnt >> (k + 3)) & 1) == 1

        @pl.when(take if enable is None else take & enable)
        def _():
            v = vmem_ref.at[pl.ds(pl.multiple_of(vrow + done, SUBLANES), size)]
            g = hbm_ref.at[pl.ds(pl.multiple_of(hrow + done, SUBLANES), size)]
            cp = pltpu.make_async_copy(v, g, sem) if to_hbm else pltpu.make_async_copy(g, v, sem)
            if wait:
                cp.wait()
            else:
                cp.start()


def _segment_copies(i, seg_ref, cnt_ref, off_ref, vmem_ref, hbm_ref, sem, t, to_hbm, wait, enable=None):
    if wait:
        extra = sum(cnt_ref[i * N_EXPERTS + e] for e in range(N_EXPERTS)) - 2 * t
        v, g = vmem_ref.at[pl.ds(0, 2 * t)], hbm_ref.at[pl.ds(0, 2 * t)]
        whole = pltpu.make_async_copy(v, g, sem) if to_hbm else pltpu.make_async_copy(g, v, sem)
        if enable is None:
            whole.wait()
        else:
            pl.when(enable)(whole.wait)
        _rows_copy(extra, 0, 0, vmem_ref, hbm_ref, sem, N_EXPERTS * SUBLANES // 2, to_hbm, True, enable)
        return
    for e in range(N_EXPERTS):
        k = i * N_EXPERTS + e
        _rows_copy(cnt_ref[k], seg_ref[k], off_ref[k], vmem_ref, hbm_ref, sem, t, to_hbm, wait, enable)


def _dispatch_kernel(seg_ref, cnt_ref, off_ref, pad_ref, h_ref, route_ref, hs_ref, comp_ref, sem, *, tm):
    i = pl.program_id(0)
    last = pl.num_programs(0) - 1
    t = h_ref.shape[0]
    rows = comp_ref.shape[1]
    slot = i % 2
    copies = functools.partial(_segment_copies, seg_ref=seg_ref, cnt_ref=cnt_ref, off_ref=off_ref,
                               hbm_ref=hs_ref, t=t, to_hbm=True)

    @pl.when(i >= 2)
    def _():
        copies(i - 2, vmem_ref=comp_ref.at[slot], sem=sem.at[slot], wait=True)

    d1, d2 = _dest_rows(route_ref[0], seg_ref, i)
    rid = lax.broadcasted_iota(jnp.int32, (rows, t), 0)
    onehot = jnp.where((rid == d1) | (rid == d2), 1.0, 0.0).astype(BF16)
    comp_ref[slot] = _dot(onehot, h_ref[...])
    copies(i, vmem_ref=comp_ref.at[slot], sem=sem.at[slot], wait=False)

    @pl.when(i == last)
    def _():
        @pl.when(i >= 1)
        def _():
            copies(i - 1, vmem_ref=comp_ref.at[1 - slot], sem=sem.at[1 - slot], wait=True)

        copies(i, vmem_ref=comp_ref.at[slot], sem=sem.at[slot], wait=True)
        comp_ref[0, 0:tm, :] = jnp.zeros((tm, comp_ref.shape[2]), F32)
        zeros = comp_ref.at[0]
        for wait in (False, True):
            for e in range(N_EXPERTS):
                _rows_copy(pad_ref[N_EXPERTS + e], 0, pad_ref[e], zeros, hs_ref, sem.at[2], tm, True, wait)

        def tail(g, carry):
            row = pl.multiple_of(g * tm, SUBLANES)
            cp = pltpu.make_async_copy(zeros.at[pl.ds(0, tm)], hs_ref.at[pl.ds(row, tm)], sem.at[2])
            cp.start()
            cp.wait()
            return carry

        lax.fori_loop(pad_ref[2 * N_EXPERTS], pad_ref[2 * N_EXPERTS + 1], tail, 0)


def _dispatch(h2d, route, seg, cnt8, off, pad, total_rows):
    n, d = h2d.shape
    t = ROUTE_T
    nt = n // t
    rows = 2 * t + N_EXPERTS * SUBLANES
    assert rows >= MOE_TM
    return pl.pallas_call(
        functools.partial(_dispatch_kernel, tm=MOE_TM),
        grid_spec=pltpu.PrefetchScalarGridSpec(
            num_scalar_prefetch=4,
            grid=(nt,),
            in_specs=[
                pl.BlockSpec((t, d), lambda i, *_: (i, 0)),
                pl.BlockSpec((1, N_EXPERTS, t), lambda i, *_: (i, 0, 0)),
            ],
            out_specs=pl.BlockSpec(memory_space=pl.ANY),
            scratch_shapes=[pltpu.VMEM((2, rows, d), F32), pltpu.SemaphoreType.DMA((3,))],
        ),
        out_shape=jax.ShapeDtypeStruct((total_rows, d), F32),
        compiler_params=pltpu.CompilerParams(
            dimension_semantics=("arbitrary",), vmem_limit_bytes=VMEM_LIMIT),
        name="dispatch",
    )(seg, cnt8, off, pad, h2d, route)


def _moe_kernel(be_ref, nv_ref, hs_ref, wg_ref, wu_ref, wd_ref, o_ref, hb_ref, *, fc):
    g = pl.program_id(0)
    f = pl.program_id(1)
    nvalid = nv_ref[g]

    @pl.when(nvalid > 0)
    def _():
        @pl.when(f == 0)
        def _():
            hb_ref[...] = hs_ref[...].astype(BF16)

        h = hb_ref[...]
        width = wg_ref.shape[2]
        acc = None
        for lo in range(0, width, fc):
            fs = slice(lo, min(lo + fc, width))
            gt = _dot(h, wg_ref[0, :, fs])
            up = _dot(h, wu_ref[0, :, fs])
            act = (gt * _sigmoid(gt) * up).astype(BF16)
            part = _dot(act, wd_ref[0, fs, :])
            acc = part if acc is None else acc + part

        @pl.when(f == 0)
        def _():
            o_ref[...] = acc

        @pl.when(f > 0)
        def _():
            o_ref[...] += acc

    @pl.when((nvalid == 0) & (f == 0))
    def _():
        o_ref[...] = jnp.zeros_like(o_ref)


def _moe(hs, blk_e, blk_nv, wg, wu, wd):
    rows = hs.shape[0]
    d = wg.shape[1]
    tm = MOE_TM
    fstep = MOE_FSTEP
    nf = wg.shape[2] // fstep
    nblk = rows // tm
    assert wg.shape[2] % fstep == 0

    def fidx(g, f, nv):
        return jnp.where(nv[g] > 0, f, nf - 1)

    return pl.pallas_call(
        functools.partial(_moe_kernel, fc=MOE_FC),
        grid_spec=pltpu.PrefetchScalarGridSpec(
            num_scalar_prefetch=2,
            grid=(nblk, nf),
            in_specs=[
                pl.BlockSpec((tm, d), lambda g, f, be, nv: (g, 0)),
                pl.BlockSpec((1, d, fstep), lambda g, f, be, nv: (be[g], 0, fidx(g, f, nv))),
                pl.BlockSpec((1, d, fstep), lambda g, f, be, nv: (be[g], 0, fidx(g, f, nv))),
                pl.BlockSpec((1, fstep, d), lambda g, f, be, nv: (be[g], fidx(g, f, nv), 0)),
            ],
            out_specs=pl.BlockSpec((tm, d), lambda g, f, be, nv: (g, 0)),
            scratch_shapes=[pltpu.VMEM((tm, d), BF16)],
        ),
        out_shape=jax.ShapeDtypeStruct((rows, d), F32),
        compiler_params=pltpu.CompilerParams(
            dimension_semantics=("arbitrary", "arbitrary"), vmem_limit_bytes=VMEM_LIMIT),
        name="moe",
    )(blk_e, blk_nv, hs, wg, wu, wd)


def _combine_kernel(seg_ref, cnt_ref, off_ref, x_ref, routec_ref, mod_ref, gfin_ref, ys_ref, o_ref,
                    comp_ref, sem):
    i = pl.program_id(0)
    t = x_ref.shape[0]
    rows = comp_ref.shape[1]
    slot = i % 2
    fetch = functools.partial(_segment_copies, seg_ref=seg_ref, cnt_ref=cnt_ref, off_ref=off_ref,
                              hbm_ref=ys_ref, t=t, to_hbm=False)

    @pl.when(i == 0)
    def _():
        comp_ref[...] = jnp.zeros_like(comp_ref)
        fetch(i, vmem_ref=comp_ref.at[slot], sem=sem.at[slot], wait=False)

    @pl.when(i + 1 < pl.num_programs(0))
    def _():
        fetch(i + 1, vmem_ref=comp_ref.at[1 - slot], sem=sem.at[1 - slot], wait=False)

    fetch(i, vmem_ref=comp_ref.at[slot], sem=sem.at[slot], wait=True)
    rc = routec_ref[0]
    i1, i2, w1, w2, r1, r2 = (rc[:, k:k + 1] for k in range(6))
    s1 = jnp.zeros_like(i1)
    s2 = jnp.zeros_like(i2)
    for e in range(N_EXPERTS):
        start = seg_ref[i * N_EXPERTS + e].astype(F32)
        s1 = jnp.where(i1 == e, start, s1)
        s2 = jnp.where(i2 == e, start, s2)
    d1 = (s1 + r1).astype(jnp.int32)
    d2 = (s2 + r2).astype(jnp.int32)
    cid = lax.broadcasted_iota(jnp.int32, (t, rows), 1)
    wmat = (jnp.where(cid == d1, w1, 0.0) + jnp.where(cid == d2, w2, 0.0)).astype(BF16)
    y = _dot(wmat, comp_ref[slot].astype(BF16))
    gate_f = mod_ref[0][5:6]
    x2 = x_ref[...] + gate_f * y
    ms = jnp.mean(x2 * x2, axis=-1, keepdims=True)
    o_ref[...] = x2 * lax.rsqrt(ms + EPS) * gfin_ref[...]


def _combine(x2d, routec, mod, gfin, ys, seg, cnt8, off, seq):
    n, d = x2d.shape
    t = ROUTE_T
    nt = n // t
    tiles_per_seq = seq // t
    rows = 2 * t + LANES
    return pl.pallas_call(
        _combine_kernel,
        grid_spec=pltpu.PrefetchScalarGridSpec(
            num_scalar_prefetch=3,
            grid=(nt,),
            in_specs=[
                pl.BlockSpec((t, d), lambda i, *_: (i, 0)),
                pl.BlockSpec((1, t, LANES), lambda i, *_: (i, 0, 0)),
                pl.BlockSpec((1,) + mod.shape[1:], lambda i, *_: (i // tiles_per_seq, 0, 0)),
                pl.BlockSpec(gfin.shape, lambda i, *_: (0, 0)),
                pl.BlockSpec(memory_space=pl.ANY),
            ],
            out_specs=pl.BlockSpec((t, d), lambda i, *_: (i, 0)),
            scratch_shapes=[pltpu.VMEM((2, rows, d), F32), pltpu.SemaphoreType.DMA((2,))],
        ),
        out_shape=jax.ShapeDtypeStruct((n, d), F32),
        compiler_params=pltpu.CompilerParams(
            dimension_semantics=("arbitrary",), vmem_limit_bytes=VMEM_LIMIT),
        name="combine",
    )(seg, cnt8, off, x2d, routec, mod, gfin, ys)


def _moe_plan(cnt, tm):
    nt = cnt.shape[0]
    cnt8 = (cnt + SUBLANES - 1) // SUBLANES * SUBLANES
    seg = jnp.cumsum(cnt8, axis=1) - cnt8
    tot = jnp.sum(cnt8, axis=0)
    nblk = (tot + tm - 1) // tm
    blk_end = jnp.cumsum(nblk)
    blk_start = blk_end - nblk
    off = (blk_start * tm)[None, :] + jnp.cumsum(cnt8, axis=0) - cnt8
    max_rows = 2 * nt * ROUTE_T + nt * N_EXPERTS * (SUBLANES - 1)
    n_blocks = (max_rows + tm - 1) // tm + N_EXPERTS
    g = jnp.arange(n_blocks, dtype=jnp.int32)
    blk_e = jnp.minimum(jnp.sum(g[:, None] >= blk_end[None, :], axis=1), N_EXPERTS - 1).astype(jnp.int32)
    blk_nv = jnp.clip(tot[blk_e] - (g - blk_start[blk_e]) * tm, 0, tm).astype(jnp.int32)
    pad = jnp.concatenate([blk_start * tm + tot, nblk * tm - tot, blk_end[-1:],
                           jnp.full((1,), n_blocks, jnp.int32)])
    flat = lambda a: a.reshape(-1).astype(jnp.int32)
    return flat(seg), flat(cnt8), flat(off), flat(pad), blk_e, blk_nv, n_blocks * tm


def kernel(x, c, ada_w, ada_b, mix_norm_g, ffn_norm_g, w_in, pool_w, pool_scale, conv_w, conv_b,
           conv_ln_g, conv_ln_b, sgu_ln_g, sgu_ln_b, sgu_w, sgu_b, w_out, ffn_w_gate, ffn_w_up,
           ffn_w_down, router_w, router_b, moe_w_gate, moe_w_up, moe_w_down, final_norm_g):
    b, s, d = x.shape
    depth = ada_w.shape[0]
    n_mod = ada_w.shape[2] // d
    mod_all = _modulation(c, ada_w, ada_b).reshape(depth, b, n_mod, d)
    row = lambda v: v.reshape(1, -1)

    for l in range(depth):
        mod = mod_all[l]
        pool_bd = jax.scipy.linalg.block_diag(*[pool_w[l, g] for g in range(pool_w.shape[1])]).astype(BF16)
        sw = sgu_w[l]
        sgu_wcat = jnp.concatenate([sw[0::2], sw[1::2]], axis=2).astype(BF16)
        sgu_bias = jnp.repeat(sgu_b[l].T, HEAD_DIM, axis=1)
        i = l // 2
        router = None
        if l % 2 == 1:
            rw_pad = jnp.pad(router_w[i], ((0, 0), (0, LANES - N_EXPERTS))).astype(BF16)
            tri = jnp.triu(jnp.ones((ROUTE_T, ROUTE_T), BF16), k=1)
            router = (rw_pad, router_b[i].reshape(N_EXPERTS, 1), tri)
        x, h, *tables = _mix_layer(x, mod, row(mix_norm_g[l]), row(ffn_norm_g[l]), w_in[l].astype(BF16),
                                   pool_bd, row(pool_scale[l]), conv_w[l], row(conv_b[l]),
                                   row(conv_ln_g[l]), row(conv_ln_b[l]), row(sgu_ln_g[l]),
                                   row(sgu_ln_b[l]), sgu_wcat, sgu_bias, w_out[l].astype(BF16), router)
        if l % 2 == 0:
            x = _dense_ffn(x, h, mod, ffn_w_gate[i].astype(BF16), ffn_w_up[i].astype(BF16),
                           ffn_w_down[i].astype(BF16))
        else:
            assert l == depth - 1, "the combine kernel applies the final norm"
            h2d = h.reshape(b * s, d)
            route, routec, cnt = tables
            seg, cnt8, off, pad, blk_e, blk_nv, total_rows = _moe_plan(
                cnt[:, :, 0].astype(jnp.int32), MOE_TM)
            hs = _dispatch(h2d, route, seg, cnt8, off, pad, total_rows)
            ys = _moe(hs, blk_e, blk_nv, moe_w_gate[i].astype(BF16), moe_w_up[i].astype(BF16),
                      moe_w_down[i].astype(BF16))
            x = _combine(x.reshape(b * s, d), routec, mod, row(final_norm_g), ys, seg, cnt8, off,
                         s).reshape(b, s, d)
    return x
```

```python
import functools

import jax
import jax.numpy as jnp
from jax import lax
from jax.experimental import pallas as pl
from jax.experimental.pallas import tpu as pltpu

EPS = 1e-6
F32 = jnp.float32
BF16 = jnp.bfloat16

SUBLANES = 8
LANES = 128
MXU_N = 256
_DONE = object()

POOL_WINDOWS = (2, 4, 8, 16)
POOL_GROUP = 64
CONV_KERNEL = 31
HEAD_DIM = 64
CHUNK = 128
N_EXPERTS = 8
HALO = 16

MIX_TM = 512
MIX_SUBTILES = 2
FFN_TM = 512
FFN_FC = 768
ROUTE_T = 512
MOE_TM = 768
MOE_FSTEP = 1792
MOE_FC = 512
CONV_STRIDE = 4
VMEM_LIMIT = 48 * 1024 * 1024


def _dot(a, b):
    return jnp.dot(a, b, preferred_element_type=F32)


def _dot_tn(a, b):
    return lax.dot_general(a, b, (((0,), (0,)), ((), ())), preferred_element_type=F32)


def _sigmoid(x):
    return 1.0 / (1.0 + jnp.exp(-x))


def _rms_mod(x, g_row, scale_row, shift_row):
    ms = jnp.mean(x * x, axis=-1, keepdims=True)
    return x * lax.rsqrt(ms + EPS) * (g_row * (1.0 + scale_row)) + shift_row


def _layer_norm(x, g_row, b_row):
    mu = jnp.mean(x, axis=-1, keepdims=True)
    xc = x - mu
    var = jnp.mean(xc * xc, axis=-1, keepdims=True)
    return xc * lax.rsqrt(var + EPS) * g_row + b_row


def _mod_kernel(c_ref, w_ref, b_ref, o_ref):
    c = c_ref[...]
    cond = (c * _sigmoid(c)).astype(BF16)
    o_ref[0] = _dot(cond, w_ref[0].astype(BF16)) + b_ref[0]


def _modulation(c, ada_w, ada_b):
    depth, d, nd = ada_w.shape
    b = c.shape[0]
    nblk = nd // d
    return pl.pallas_call(
        _mod_kernel,
        grid=(depth, nblk),
        in_specs=[
            pl.BlockSpec((b, d), lambda l, n: (0, 0)),
            pl.BlockSpec((1, d, d), lambda l, n: (l, 0, n)),
            pl.BlockSpec((1, 1, d), lambda l, n: (l, 0, n)),
        ],
        out_specs=pl.BlockSpec((1, b, d), lambda l, n: (l, 0, n)),
        out_shape=jax.ShapeDtypeStruct((depth, b, nd), F32),
        name="mod",
    )(c, ada_w, ada_b.reshape(depth, 1, nd))


def _route_tables(h, rw, rb, tri):
    t = h.shape[0]
    logits = _dot(h, rw)
    lt = logits.T[0:N_EXPERTS] + rb
    eid = lax.broadcasted_iota(jnp.int32, (N_EXPERTS, t), 0).astype(F32)
    m1 = jnp.max(lt, axis=0, keepdims=True)
    i1 = jnp.min(jnp.where(lt == m1, eid, float(N_EXPERTS)), axis=0, keepdims=True)
    sel1 = eid == i1
    lt2 = jnp.where(sel1, -jnp.inf, lt)
    m2 = jnp.max(lt2, axis=0, keepdims=True)
    i2 = jnp.min(jnp.where(lt2 == m2, eid, float(N_EXPERTS)), axis=0, keepdims=True)
    sel2 = eid == i2
    e21 = jnp.exp(m2 - m1)
    w1 = 1.0 / (1.0 + e21)
    w2 = e21 / (1.0 + e21)
    count = jnp.where(sel1 | sel2, 1.0, 0.0)
    rank = _dot(count.astype(BF16), tri)
    r1 = jnp.sum(jnp.where(sel1, rank, 0.0), axis=0, keepdims=True)
    r2 = jnp.sum(jnp.where(sel2, rank, 0.0), axis=0, keepdims=True)
    out = jnp.zeros((N_EXPERTS, t), F32)
    for k, rowv in enumerate((i1, i2, w1, w2, r1, r2)):
        out = jnp.where(eid == float(k), rowv, out)
    return out, jnp.sum(count, axis=1, keepdims=True)


def _mix_kernel(xm_ref, xp_ref, xn_ref, mod_ref, g1_ref, g2_ref, win_ref, pbd_ref, pscale_ref,
                cw_ref, cb_ref, clg_ref, clb_ref, slg_ref, slb_ref, swcat_ref, sbias_ref, wout_ref,
                *rest, tm, seq, n_sub, with_route):
    if with_route:
        rw_ref, rb_ref, tri_ref, xo_ref, ho_ref, route_ref, routec_ref, cnt_ref = rest[:8]
    else:
        xo_ref, ho_ref = rest[:2]
    glu_ref, ystage_ref, mixed_ref = rest[-3:]
    j = pl.program_id(1)
    nj = pl.num_programs(1)
    st = tm // n_sub
    ext = st + 2 * HALO
    pool_w = pbd_ref.shape[0]
    conv_w = cw_ref.shape[1]
    sgu_w = slg_ref.shape[1]
    pc_w = pool_w + 2 * conv_w
    n_slab = conv_w // LANES
    span = SUBLANES * CONV_STRIDE

    mod = mod_ref[0]
    shift_m, scale_m, gate_m = mod[0:1], mod[1:2], mod[2:3]
    shift_f, scale_f = mod[3:4], mod[4:5]
    cb = [cb_ref[:, m * LANES:(m + 1) * LANES] for m in range(n_slab)]
    clg = [clg_ref[:, m * LANES:(m + 1) * LANES] for m in range(n_slab)]
    clb = [clb_ref[:, m * LANES:(m + 1) * LANES] for m in range(n_slab)]
    lane = lax.broadcasted_iota(jnp.int32, (1, pool_w), 1)
    half = jnp.where(lane < POOL_GROUP, 1,
                     jnp.where(lane < 2 * POOL_GROUP, 2, jnp.where(lane < 3 * POOL_GROUP, 4, 8)))
    inv_full = 1.0 / (2 * half).astype(F32)

    def edge_inv(t0):
        t = lax.broadcasted_iota(jnp.int32, (SUBLANES, pool_w), 0) + t0
        return 1.0 / (jnp.minimum(t + half, seq) - jnp.maximum(t - half, 0)).astype(F32)

    state = {}

    def project(sub):
        r0 = sub * st
        first, last = sub == 0, sub == n_sub - 1
        xm = xm_ref[0, r0:r0 + st, :]
        xprev = xp_ref[0] if first else xm_ref[0, r0 - HALO:r0, :]
        xnext = xn_ref[0] if last else xm_ref[0, r0 + st:r0 + st + HALO, :]
        xe = jnp.concatenate([xprev, xm, xnext], axis=0)
        hn = _rms_mod(xe, g1_ref[...], scale_m, shift_m)
        if first or last:
            r = lax.broadcasted_iota(jnp.int32, (ext, 1), 0)
            ok = None
            if first:
                ok = (r >= HALO) | (j > 0)
            if last:
                ok_n = (r < HALO + st) | (j < nj - 1)
                ok = ok_n if ok is None else ok & ok_n
            hn = jnp.where(ok, hn, 0.0)
        hb = hn.astype(BF16)

        cols = []
        for c in range(0, pc_w, MXU_N):
            cols.append(_dot(hb, win_ref[:, c:c + MXU_N]))
            yield
        p1 = jnp.concatenate(cols, axis=1)
        hbm = hb[HALO:HALO + st]
        cols = []
        for c in range(pc_w, pc_w + 2 * sgu_w, MXU_N):
            cols.append(_dot(hbm, win_ref[:, c:c + MXU_N]))
            yield
        p2 = jnp.concatenate(cols, axis=1)
        val = p1[:, pool_w:pool_w + conv_w]
        gate = p1[:, pool_w + conv_w:pc_w]
        glu = val * _sigmoid(gate)
        for m in range(n_slab):
            glu_ref[sub, m] = glu[:, m * LANES:(m + 1) * LANES]
        state[sub] = (xm, p1[:, 0:pool_w], p2)

    def mixers(sub):
        r0 = sub * st
        first, last = sub == 0, sub == n_sub - 1
        xm, a, p2 = state[sub]

        s2 = a[0:ext - 1] + a[1:ext]
        s4 = s2[0:ext - 3] + s2[2:ext - 1]
        a_hi = s4[:, LANES:pool_w]
        s8 = a_hi[0:ext - 7] + a_hi[4:ext - 3]
        s16 = s8[0:ext - 15] + s8[8:ext - 7]
        is_first_group = lax.broadcasted_iota(jnp.int32, (st, LANES), 1) < POOL_GROUP
        win_lo = jnp.where(is_first_group, s2[HALO - 1:HALO - 1 + st, 0:LANES],
                           s4[HALO - 2:HALO - 2 + st, 0:LANES])
        win_hi = jnp.where(is_first_group, s8[HALO - 4:HALO - 4 + st], s16[HALO - 8:HALO - 8 + st])
        win = jnp.concatenate([win_lo, win_hi], axis=1)
        am = a[HALO:HALO + st]
        lo_rows = SUBLANES if first else 0
        hi_rows = SUBLANES if last else 0
        pieces = []
        if first:
            pieces.append(win[0:SUBLANES] * edge_inv(j * tm + r0) - am[0:SUBLANES])
        pieces.append(win[lo_rows:st - hi_rows] * inv_full - am[lo_rows:st - hi_rows])
        if last:
            pieces.append(win[st - SUBLANES:st] * edge_inv(j * tm + r0 + st - SUBLANES) - am[st - SUBLANES:st])
        diff = jnp.concatenate(pieces, axis=0) if len(pieces) > 1 else pieces[0]
        out_a = _dot(diff.astype(BF16), pbd_ref[...]) * pscale_ref[...]
        mixed_ref[r0:r0 + st, 0:pool_w] = out_a.astype(BF16)
        yield

        for blk in range(st // span):
            row0 = blk * span
            acc = [[None] * CONV_STRIDE for _ in range(n_slab)]
            for m in range(n_slab):
                w = {}
                for a in range(CONV_KERNEL + CONV_STRIDE - 1):
                    if a < CONV_KERNEL:
                        w[a] = cw_ref[a:a + 1, m * LANES:(m + 1) * LANES]
                    start = row0 + HALO - CONV_KERNEL // 2 + a
                    rows = glu_ref[sub, m, pl.ds(start, SUBLANES, stride=CONV_STRIDE), :]
                    for s in range(CONV_STRIDE):
                        k = a - s
                        if 0 <= k < CONV_KERNEL:
                            term = rows * w[k]
                            acc[m][s] = term if k == 0 else acc[m][s] + term
            for s in range(CONV_STRIDE):
                ys = [acc[m][s] + cb[m] for m in range(n_slab)]
                mu = sum(jnp.sum(y, axis=-1, keepdims=True) for y in ys) * (1.0 / conv_w)
                yc = [y - mu for y in ys]
                var = sum(jnp.sum(y * y, axis=-1, keepdims=True) for y in yc) * (1.0 / conv_w)
                rstd = lax.rsqrt(var + EPS)
                for m in range(n_slab):
                    yn = yc[m] * rstd * clg[m] + clb[m]
                    ystage_ref[sub, m, pl.ds(row0 + s, SUBLANES, stride=CONV_STRIDE), :] = yn * _sigmoid(yn)
            yield
        for m in range(n_slab):
            mixed_ref[r0:r0 + st, pool_w + m * LANES:pool_w + (m + 1) * LANES] = ystage_ref[sub, m].astype(BF16)

        u = p2[:, 0:sgu_w]
        v = _layer_norm(p2[:, sgu_w:2 * sgu_w], slg_ref[...], slb_ref[...])
        is_lo = (lax.broadcasted_iota(jnp.int32, (st, sgu_w), 1) % LANES) < HEAD_DIM
        v_lo = jnp.where(is_lo, v, 0.0).astype(BF16)
        v_hi = jnp.where(is_lo, 0.0, v).astype(BF16)
        sbias = sbias_ref[...]
        col0 = pool_w + conv_w
        for cp in range(st // (2 * CHUNK)):
            ra = 2 * cp * CHUNK
            rb = ra + CHUNK
            for m in range(sgu_w // LANES):
                ls = slice(m * LANES, (m + 1) * LANES)
                rhs = jnp.concatenate([
                    jnp.concatenate([v_lo[ra:ra + CHUNK, ls], v_lo[rb:rb + CHUNK, ls]], axis=1),
                    jnp.concatenate([v_hi[ra:ra + CHUNK, ls], v_hi[rb:rb + CHUNK, ls]], axis=1)], axis=0)
                mixed = _dot(swcat_ref[m], rhs)
                bias = sbias[:, ls]
                oa = u[ra:ra + CHUNK, ls] * (mixed[:, 0:LANES] + bias)
                ob = u[rb:rb + CHUNK, ls] * (mixed[:, LANES:2 * LANES] + bias)
                cs = slice(col0 + m * LANES, col0 + (m + 1) * LANES)
                mixed_ref[r0 + ra:r0 + ra + CHUNK, cs] = oa.astype(BF16)
                mixed_ref[r0 + rb:r0 + rb + CHUNK, cs] = ob.astype(BF16)
                yield

    def project_out(sub):
        r0 = sub * st
        xm = state[sub][0]
        mixed = mixed_ref[r0:r0 + st, :]
        cols = []
        for c in range(0, wout_ref.shape[1], MXU_N):
            res = _dot(mixed, wout_ref[:, c:c + MXU_N])
            cols.append(xm[:, c:c + MXU_N] + gate_m[:, c:c + MXU_N] * res)
            yield
        xnew = jnp.concatenate(cols, axis=1)
        xo_ref[0, r0:r0 + st, :] = xnew
        ho_ref[0, r0:r0 + st, :] = _rms_mod(xnew, g2_ref[...], scale_f, shift_f).astype(BF16)

    def emit(*stages):
        live = list(stages)
        while live:
            for gen in list(live):
                if next(gen, _DONE) is _DONE:
                    live.remove(gen)

    for step in range(n_sub + 2):
        emit(*([project_out(step - 2)] if 0 <= step - 2 < n_sub else []),
             *([mixers(step - 1)] if 0 <= step - 1 < n_sub else []),
             *([project(step)] if step < n_sub else []))

    if with_route:
        table, cnt = _route_tables(ho_ref[0], rw_ref[...], rb_ref[...], tri_ref[...])
        route_ref[0] = table
        routec_ref[0] = jnp.concatenate([table, jnp.zeros((LANES - N_EXPERTS, tm), F32)], axis=0).T
        cnt_ref[0] = jnp.broadcast_to(cnt, (N_EXPERTS, LANES))


def _mix_layer(x, mod, g1, g2, w_in, pool_bd, pool_scale, conv_w, conv_b, conv_ln_g, conv_ln_b,
               sgu_ln_g, sgu_ln_b, sgu_wcat, sgu_bias, w_out, router=None):
    b, s, d = x.shape
    tm = MIX_TM
    n_sub = MIX_SUBTILES
    st = tm // n_sub
    assert s % tm == 0 and tm % n_sub == 0 and st % (2 * CHUNK) == 0 and st % (SUBLANES * CONV_STRIDE) == 0
    n_slab = conv_w.shape[1] // LANES
    hb = tm // HALO
    nj = s // tm
    last_halo = s // HALO - 1
    full = lambda arr: pl.BlockSpec(arr.shape, lambda bi, j: (0,) * arr.ndim)
    params = [g1, g2, w_in, pool_bd, pool_scale, conv_w, conv_b, conv_ln_g, conv_ln_b,
              sgu_ln_g, sgu_ln_b, sgu_wcat, sgu_bias, w_out]
    out_specs = [
        pl.BlockSpec((1, tm, d), lambda bi, j: (bi, j, 0)),
        pl.BlockSpec((1, tm, d), lambda bi, j: (bi, j, 0)),
    ]
    out_shape = [jax.ShapeDtypeStruct((b, s, d), F32), jax.ShapeDtypeStruct((b, s, d), BF16)]
    if router is not None:
        assert tm == ROUTE_T
        params += list(router)
        tile = lambda bi, j: (bi * nj + j, 0, 0)
        out_specs += [pl.BlockSpec((1, N_EXPERTS, tm), tile), pl.BlockSpec((1, tm, LANES), tile),
                      pl.BlockSpec((1, N_EXPERTS, LANES), tile)]
        out_shape += [jax.ShapeDtypeStruct((b * nj, N_EXPERTS, tm), F32),
                      jax.ShapeDtypeStruct((b * nj, tm, LANES), F32),
                      jax.ShapeDtypeStruct((b * nj, N_EXPERTS, LANES), F32)]
    return pl.pallas_call(
        functools.partial(_mix_kernel, tm=tm, seq=s, n_sub=n_sub, with_route=router is not None),
        grid=(b, nj),
        in_specs=[
            pl.BlockSpec((1, tm, d), lambda bi, j: (bi, j, 0)),
            pl.BlockSpec((1, HALO, d), lambda bi, j: (bi, jnp.maximum(j * hb - 1, 0), 0)),
            pl.BlockSpec((1, HALO, d), lambda bi, j: (bi, jnp.minimum((j + 1) * hb, last_halo), 0)),
            pl.BlockSpec((1,) + mod.shape[1:], lambda bi, j: (bi, 0, 0)),
        ] + [full(p) for p in params],
        out_specs=out_specs,
        out_shape=out_shape,
        scratch_shapes=[
            pltpu.VMEM((n_sub, n_slab, st + 2 * HALO, LANES), F32),
            pltpu.VMEM((n_sub, n_slab, st, LANES), F32),
            pltpu.VMEM((tm, d), BF16),
        ],
        compiler_params=pltpu.CompilerParams(
            dimension_semantics=("arbitrary", "arbitrary"), vmem_limit_bytes=VMEM_LIMIT),
        name="mix",
    )(x, x, x, mod, *params)


def _ffn_kernel(x_ref, h_ref, mod_ref, wg_ref, wu_ref, wd_ref, o_ref, *, fc):
    h = h_ref[0]
    gate_f = mod_ref[0][5:6]
    width = wg_ref.shape[1]
    acc = None
    for lo in range(0, width, fc):
        fs = slice(lo, min(lo + fc, width))
        g = _dot(h, wg_ref[:, fs])
        u = _dot(h, wu_ref[:, fs])
        act = (g * _sigmoid(g) * u).astype(BF16)
        part = _dot(act, wd_ref[fs, :])
        acc = part if acc is None else acc + part
    o_ref[0] = x_ref[0] + gate_f * acc


def _dense_ffn(x, h, mod, wg, wu, wd):
    b, s, d = x.shape
    tm = FFN_TM
    assert s % tm == 0
    full = lambda arr: pl.BlockSpec(arr.shape, lambda bi, j: (0,) * arr.ndim)
    return pl.pallas_call(
        functools.partial(_ffn_kernel, fc=FFN_FC),
        grid=(b, s // tm),
        in_specs=[
            pl.BlockSpec((1, tm, d), lambda bi, j: (bi, j, 0)),
            pl.BlockSpec((1, tm, d), lambda bi, j: (bi, j, 0)),
            pl.BlockSpec((1,) + mod.shape[1:], lambda bi, j: (bi, 0, 0)),
            full(wg), full(wu), full(wd),
        ],
        out_specs=pl.BlockSpec((1, tm, d), lambda bi, j: (bi, j, 0)),
        out_shape=jax.ShapeDtypeStruct((b, s, d), F32),
        compiler_params=pltpu.CompilerParams(
            dimension_semantics=("arbitrary", "arbitrary"), vmem_limit_bytes=VMEM_LIMIT),
        name="ffn",
    )(x, h, mod, wg, wu, wd)


def _dest_rows(route, seg_ref, i):
    i1, i2, r1, r2 = route[0:1], route[1:2], route[4:5], route[5:6]
    s1 = jnp.zeros_like(i1)
    s2 = jnp.zeros_like(i2)
    for e in range(N_EXPERTS):
        start = seg_ref[i * N_EXPERTS + e].astype(F32)
        s1 = jnp.where(i1 == e, start, s1)
        s2 = jnp.where(i2 == e, start, s2)
    return (s1 + r1).astype(jnp.int32), (s2 + r2).astype(jnp.int32)


def _rows_copy(cnt, vrow, hrow, vmem_ref, hbm_ref, sem, max_rows, to_hbm, wait, enable=None, queue=0):
    for k in range((max_rows // SUBLANES).bit_length()):
        size = SUBLANES << k
        done = (cnt >> (k + 4)) << (k + 4)
        take = ((cnt >> (k + 3)) & 1) == 1

        @pl.when(take if enable is None else take & enable)
        def _():
            v = vmem_ref.at[pl.ds(pl.multiple_of(vrow + done, SUBLANES), size)]
            g = hbm_ref.at[pl.ds(pl.multiple_of(hrow + done, SUBLANES), size)]
            cp = pltpu.make_async_copy(v, g, sem) if to_hbm else pltpu.make_async_copy(g, v, sem)
            if wait:
                cp.wait()
            else:
                cp.start(priority=(queue + k) % 2)


def _segment_copies(i, seg_ref, cnt_ref, off_ref, vmem_ref, hbm_ref, sem, t, to_hbm, wait, enable=None):
    if wait:
        extra = sum(cnt_ref[i * N_EXPERTS + e] for e in range(N_EXPERTS)) - 2 * t
        v, g = vmem_ref.at[pl.ds(0, 2 * t)], hbm_ref.at[pl.ds(0, 2 * t)]
        whole = pltpu.make_async_copy(v, g, sem) if to_hbm else pltpu.make_async_copy(g, v, sem)
        if enable is None:
            whole.wait()
        else:
            pl.when(enable)(whole.wait)
        _rows_copy(extra, 0, 0, vmem_ref, hbm_ref, sem, N_EXPERTS * SUBLANES // 2, to_hbm, True, enable)
        return
    for e in range(N_EXPERTS):
        k = i * N_EXPERTS + e
        _rows_copy(cnt_ref[k], seg_ref[k], off_ref[k], vmem_ref, hbm_ref, sem, t, to_hbm, wait, enable, queue=e)


def _dispatch_kernel(seg_ref, cnt_ref, off_ref, pad_ref, h_ref, route_ref, hs_ref, comp_ref, sem, *, tm):
    i = pl.program_id(0)
    last = pl.num_programs(0) - 1
    t = h_ref.shape[0]
    rows = comp_ref.shape[1]
    slot = i % 2
    copies = functools.partial(_segment_copies, seg_ref=seg_ref, cnt_ref=cnt_ref, off_ref=off_ref,
                               hbm_ref=hs_ref, t=t, to_hbm=True)

    @pl.when(i >= 2)
    def _():
        copies(i - 2, vmem_ref=comp_ref.at[slot], sem=sem.at[slot], wait=True)

    d1, d2 = _dest_rows(route_ref[0], seg_ref, i)
    rid = lax.broadcasted_iota(jnp.int32, (rows, t), 0)
    onehot = jnp.where((rid == d1) | (rid == d2), 1.0, 0.0).astype(BF16)
    comp_ref[slot] = _dot(onehot, h_ref[...])
    copies(i, vmem_ref=comp_ref.at[slot], sem=sem.at[slot], wait=False)

    @pl.when(i == last)
    def _():
        @pl.when(i >= 1)
        def _():
            copies(i - 1, vmem_ref=comp_ref.at[1 - slot], sem=sem.at[1 - slot], wait=True)

        copies(i, vmem_ref=comp_ref.at[slot], sem=sem.at[slot], wait=True)
        comp_ref[0, 0:tm, :] = jnp.zeros((tm, comp_ref.shape[2]), F32)
        zeros = comp_ref.at[0]
        for wait in (False, True):
            for e in range(N_EXPERTS):
                _rows_copy(pad_ref[N_EXPERTS + e], 0, pad_ref[e], zeros, hs_ref, sem.at[2], tm, True, wait)

        def tail(g, carry):
            row = pl.multiple_of(g * tm, SUBLANES)
            cp = pltpu.make_async_copy(zeros.at[pl.ds(0, tm)], hs_ref.at[pl.ds(row, tm)], sem.at[2])
            cp.start()
            cp.wait()
            return carry

        lax.fori_loop(pad_ref[2 * N_EXPERTS], pad_ref[2 * N_EXPERTS + 1], tail, 0)


def _dispatch(h2d, route, seg, cnt8, off, pad, total_rows):
    n, d = h2d.shape
    t = ROUTE_T
    nt = n // t
    rows = 2 * t + N_EXPERTS * SUBLANES
    assert rows >= MOE_TM
    return pl.pallas_call(
        functools.partial(_dispatch_kernel, tm=MOE_TM),
        grid_spec=pltpu.PrefetchScalarGridSpec(
            num_scalar_prefetch=4,
            grid=(nt,),
            in_specs=[
                pl.BlockSpec((t, d), lambda i, *_: (i, 0)),
                pl.BlockSpec((1, N_EXPERTS, t), lambda i, *_: (i, 0, 0)),
            ],
            out_specs=pl.BlockSpec(memory_space=pl.ANY),
            scratch_shapes=[pltpu.VMEM((2, rows, d), F32), pltpu.SemaphoreType.DMA((3,))],
        ),
        out_shape=jax.ShapeDtypeStruct((total_rows, d), F32),
        compiler_params=pltpu.CompilerParams(
            dimension_semantics=("arbitrary",), vmem_limit_bytes=VMEM_LIMIT),
        name="dispatch",
    )(seg, cnt8, off, pad, h2d, route)


def _moe_kernel(be_ref, nv_ref, hs_ref, wg_ref, wu_ref, wd_ref, o_ref, hb_ref, *, fc):
    g = pl.program_id(0)
    f = pl.program_id(1)
    nvalid = nv_ref[g]

    @pl.when(nvalid > 0)
    def _():
        @pl.when(f == 0)
        def _():
            hb_ref[...] = hs_ref[...].astype(BF16)

        h = hb_ref[...]
        width = wg_ref.shape[2]
        acc = None
        for lo in range(0, width, fc):
            fs = slice(lo, min(lo + fc, width))
            gt = _dot(h, wg_ref[0, :, fs])
            up = _dot(h, wu_ref[0, :, fs])
            act = (gt * _sigmoid(gt) * up).astype(BF16)
            part = _dot(act, wd_ref[0, fs, :])
            acc = part if acc is None else acc + part

        @pl.when(f == 0)
        def _():
            o_ref[...] = acc

        @pl.when(f > 0)
        def _():
            o_ref[...] += acc

    @pl.when((nvalid == 0) & (f == 0))
    def _():
        o_ref[...] = jnp.zeros_like(o_ref)


def _moe(hs, blk_e, blk_nv, wg, wu, wd):
    rows = hs.shape[0]
    d = wg.shape[1]
    tm = MOE_TM
    fstep = MOE_FSTEP
    nf = wg.shape[2] // fstep
    nblk = rows // tm
    assert wg.shape[2] % fstep == 0

    def fidx(g, f, nv):
        return jnp.where(nv[g] > 0, f, nf - 1)

    return pl.pallas_call(
        functools.partial(_moe_kernel, fc=MOE_FC),
        grid_spec=pltpu.PrefetchScalarGridSpec(
            num_scalar_prefetch=2,
            grid=(nblk, nf),
            in_specs=[
                pl.BlockSpec((tm, d), lambda g, f, be, nv: (g, 0)),
                pl.BlockSpec((1, d, fstep), lambda g, f, be, nv: (be[g], 0, fidx(g, f, nv))),
                pl.BlockSpec((1, d, fstep), lambda g, f, be, nv: (be[g], 0, fidx(g, f, nv))),
                pl.BlockSpec((1, fstep, d), lambda g, f, be, nv: (be[g], fidx(g, f, nv), 0)),
            ],
            out_specs=pl.BlockSpec((tm, d), lambda g, f, be, nv: (g, 0)),
            scratch_shapes=[pltpu.VMEM((tm, d), BF16)],
        ),
        out_shape=jax.ShapeDtypeStruct((rows, d), F32),
        compiler_params=pltpu.CompilerParams(
            dimension_semantics=("arbitrary", "arbitrary"), vmem_limit_bytes=VMEM_LIMIT),
        name="moe",
    )(blk_e, blk_nv, hs, wg, wu, wd)


def _combine_kernel(seg_ref, cnt_ref, off_ref, x_ref, routec_ref, mod_ref, gfin_ref, ys_ref, o_ref,
                    comp_ref, sem):
    i = pl.program_id(0)
    t = x_ref.shape[0]
    rows = comp_ref.shape[1]
    slot = i % 2
    fetch = functools.partial(_segment_copies, seg_ref=seg_ref, cnt_ref=cnt_ref, off_ref=off_ref,
                              hbm_ref=ys_ref, t=t, to_hbm=False)

    @pl.when(i == 0)
    def _():
        comp_ref[...] = jnp.zeros_like(comp_ref)
        fetch(i, vmem_ref=comp_ref.at[slot], sem=sem.at[slot], wait=False)

    @pl.when(i + 1 < pl.num_programs(0))
    def _():
        fetch(i + 1, vmem_ref=comp_ref.at[1 - slot], sem=sem.at[1 - slot], wait=False)

    fetch(i, vmem_ref=comp_ref.at[slot], sem=sem.at[slot], wait=True)
    rc = routec_ref[0]
    i1, i2, w1, w2, r1, r2 = (rc[:, k:k + 1] for k in range(6))
    s1 = jnp.zeros_like(i1)
    s2 = jnp.zeros_like(i2)
    for e in range(N_EXPERTS):
        start = seg_ref[i * N_EXPERTS + e].astype(F32)
        s1 = jnp.where(i1 == e, start, s1)
        s2 = jnp.where(i2 == e, start, s2)
    d1 = (s1 + r1).astype(jnp.int32)
    d2 = (s2 + r2).astype(jnp.int32)
    cid = lax.broadcasted_iota(jnp.int32, (t, rows), 1)
    wmat = (jnp.where(cid == d1, w1, 0.0) + jnp.where(cid == d2, w2, 0.0)).astype(BF16)
    y = _dot(wmat, comp_ref[slot].astype(BF16))
    gate_f = mod_ref[0][5:6]
    x2 = x_ref[...] + gate_f * y
    ms = jnp.mean(x2 * x2, axis=-1, keepdims=True)
    o_ref[...] = x2 * lax.rsqrt(ms + EPS) * gfin_ref[...]


def _combine(x2d, routec, mod, gfin, ys, seg, cnt8, off, seq):
    n, d = x2d.shape
    t = ROUTE_T
    nt = n // t
    tiles_per_seq = seq // t
    rows = 2 * t + LANES
    return pl.pallas_call(
        _combine_kernel,
        grid_spec=pltpu.PrefetchScalarGridSpec(
            num_scalar_prefetch=3,
            grid=(nt,),
            in_specs=[
                pl.BlockSpec((t, d), lambda i, *_: (i, 0)),
                pl.BlockSpec((1, t, LANES), lambda i, *_: (i, 0, 0)),
                pl.BlockSpec((1,) + mod.shape[1:], lambda i, *_: (i // tiles_per_seq, 0, 0)),
                pl.BlockSpec(gfin.shape, lambda i, *_: (0, 0)),
                pl.BlockSpec(memory_space=pl.ANY),
            ],
            out_specs=pl.BlockSpec((t, d), lambda i, *_: (i, 0)),
            scratch_shapes=[pltpu.VMEM((2, rows, d), F32), pltpu.SemaphoreType.DMA((2,))],
        ),
        out_shape=jax.ShapeDtypeStruct((n, d), F32),
        compiler_params=pltpu.CompilerParams(
            dimension_semantics=("arbitrary",), vmem_limit_bytes=VMEM_LIMIT),
        name="combine",
    )(seg, cnt8, off, x2d, routec, mod, gfin, ys)


def _moe_plan(cnt, tm):
    nt = cnt.shape[0]
    cnt8 = (cnt + SUBLANES - 1) // SUBLANES * SUBLANES
    seg = jnp.cumsum(cnt8, axis=1) - cnt8
    tot = jnp.sum(cnt8, axis=0)
    nblk = (tot + tm - 1) // tm
    blk_end = jnp.cumsum(nblk)
    blk_start = blk_end - nblk
    off = (blk_start * tm)[None, :] + jnp.cumsum(cnt8, axis=0) - cnt8
    max_rows = 2 * nt * ROUTE_T + nt * N_EXPERTS * (SUBLANES - 1)
    n_blocks = (max_rows + tm - 1) // tm + N_EXPERTS
    g = jnp.arange(n_blocks, dtype=jnp.int32)
    blk_e = jnp.minimum(jnp.sum(g[:, None] >= blk_end[None, :], axis=1), N_EXPERTS - 1).astype(jnp.int32)
    blk_nv = jnp.clip(tot[blk_e] - (g - blk_start[blk_e]) * tm, 0, tm).astype(jnp.int32)
    pad = jnp.concatenate([blk_start * tm + tot, nblk * tm - tot, blk_end[-1:],
                           jnp.full((1,), n_blocks, jnp.int32)])
    flat = lambda a: a.reshape(-1).astype(jnp.int32)
    return flat(seg), flat(cnt8), flat(off), flat(pad), blk_e, blk_nv, n_blocks * tm


def kernel(x, c, ada_w, ada_b, mix_norm_g, ffn_norm_g, w_in, pool_w, pool_scale, conv_w, conv_b,
           conv_ln_g, conv_ln_b, sgu_ln_g, sgu_ln_b, sgu_w, sgu_b, w_out, ffn_w_gate, ffn_w_up,
           ffn_w_down, router_w, router_b, moe_w_gate, moe_w_up, moe_w_down, final_norm_g):
    b, s, d = x.shape
    depth = ada_w.shape[0]
    n_mod = ada_w.shape[2] // d
    mod_all = _modulation(c, ada_w, ada_b).reshape(depth, b, n_mod, d)
    row = lambda v: v.reshape(1, -1)

    for l in range(depth):
        mod = mod_all[l]
        pool_bd = jax.scipy.linalg.block_diag(*[pool_w[l, g] for g in range(pool_w.shape[1])]).astype(BF16)
        sw = sgu_w[l]
        sgu_wcat = jnp.concatenate([sw[0::2], sw[1::2]], axis=2).astype(BF16)
        sgu_bias = jnp.repeat(sgu_b[l].T, HEAD_DIM, axis=1)
        i = l // 2
        router = None
        if l % 2 == 1:
            rw_pad = jnp.pad(router_w[i], ((0, 0), (0, LANES - N_EXPERTS))).astype(BF16)
            tri = jnp.triu(jnp.ones((ROUTE_T, ROUTE_T), BF16), k=1)
            router = (rw_pad, router_b[i].reshape(N_EXPERTS, 1), tri)
        x, h, *tables = _mix_layer(x, mod, row(mix_norm_g[l]), row(ffn_norm_g[l]), w_in[l].astype(BF16),
                                   pool_bd, row(pool_scale[l]), conv_w[l], row(conv_b[l]),
                                   row(conv_ln_g[l]), row(conv_ln_b[l]), row(sgu_ln_g[l]),
                                   row(sgu_ln_b[l]), sgu_wcat, sgu_bias, w_out[l].astype(BF16), router)
        if l % 2 == 0:
            x = _dense_ffn(x, h, mod, ffn_w_gate[i].astype(BF16), ffn_w_up[i].astype(BF16),
                           ffn_w_down[i].astype(BF16))
        else:
            assert l == depth - 1, "the combine kernel applies the final norm"
            h2d = h.reshape(b * s, d)
            route, routec, cnt = tables
            seg, cnt8, off, pad, blk_e, blk_nv, total_rows = _moe_plan(
                cnt[:, :, 0].astype(jnp.int32), MOE_TM)
            hs = _dispatch(h2d, route, seg, cnt8, off, pad, total_rows)
            ys = _moe(hs, blk_e, blk_nv, moe_w_gate[i].astype(BF16), moe_w_up[i].astype(BF16),
                      moe_w_down[i].astype(BF16))
            x = _combine(x.reshape(b * s, d), routec, mod, row(final_norm_g), ys, seg, cnt8, off,
                         s).reshape(b, s, d)
    return x
```
